```python
import math
import jax, jax.numpy as jnp
from jax import lax
import numpy as np

D_MODEL = 4096
BATCH = 4
SEQ = 2048
DEPTH = 2
DEC_BATCH = 4
DEC_SEQ = 4096
PAST_LEN = 128

GRID_W = 64
HEAD_DIM = 128
BRANCH_W = 1024
N_BRANCH = 3
A_GROUPS = ((128, 1), (512, 4), (2048, 16))
A_HEADS = 8
A_QB = 64
B_HEADS = 8
NA_KR = 8
NA_KC = 16
NA_QCB = 16
NA_KSW = 32
HY_ORDER = 2
HY_CH = BRANCH_W
HY_BANDS = 16
HY_EMB = 1 + 2 * HY_BANDS
HY_FFN = 64
HY_CONV = 3
RMS_EPS = 1e-6
NEG_INF = -1e30

A_QKV = 3 * len(A_GROUPS) * A_HEADS * HEAD_DIM
B_QKV = 3 * B_HEADS * HEAD_DIM
C_PROJ = (HY_ORDER + 1) * HY_CH
OFF_AZ = A_QKV
OFF_B = OFF_AZ + BRANCH_W
OFF_BZ = OFF_B + B_QKV
OFF_C = OFF_BZ + BRANCH_W
OFF_CZ = OFF_C + C_PROJ
OFF_G = OFF_CZ + BRANCH_W
IN_COLS = OFF_G + N_BRANCH * D_MODEL

kernel_name = "hybrid_dilated_na_hyena_encoder"


def rms_norm(x, g):
    xf = x.astype(jnp.float32)
    y = xf * lax.rsqrt(jnp.mean(xf * xf, axis=-1, keepdims=True) + RMS_EPS)
    return (y * g.astype(jnp.float32)).astype(x.dtype)


def alibi_slopes(n):
    return jnp.asarray(2.0 ** (-8.0 * (np.arange(n) + 1) / n), jnp.float32)


def dilated_window_attention(q, k, v, dilation, radius, slopes):
    bsz, seq, nh, hd = q.shape
    n = seq // dilation
    nbat = bsz * dilation

    def to_residues(t):
        return t.reshape(bsz, n, dilation, nh, hd).transpose(0, 2, 1, 3, 4).reshape(nbat, n, nh, hd)

    def from_residues(t):
        tail = t.shape[2:]
        t = t.reshape((bsz, dilation, n) + tail)
        t = t.transpose((0, 2, 1) + tuple(range(3, t.ndim)))
        return t.reshape((bsz, seq) + tail)

    nb = -(-n // A_QB)
    n_pad = nb * A_QB
    slab = A_QB + 2 * radius
    starts = np.arange(nb) * A_QB
    idx = starts[:, None] + np.arange(slab)[None, :]
    qpos = starts[:, None] + np.arange(A_QB)[None, :]
    kpos = idx - radius
    rel = kpos[:, None, :] - qpos[:, :, None]
    valid = (np.abs(rel) <= radius) & (kpos[:, None, :] >= 0) & (kpos[:, None, :] < n)
    dist = (dilation * np.abs(rel)).astype(np.float32)

    qb = jnp.pad(to_residues(q), ((0, 0), (0, n_pad - n), (0, 0), (0, 0))).reshape(nbat, nb, A_QB, nh, hd)
    kv_pad = ((0, 0), (radius, n_pad - n + radius), (0, 0), (0, 0))
    kb = jnp.pad(to_residues(k), kv_pad)[:, idx]
    vb = jnp.pad(to_residues(v), kv_pad)[:, idx]

    s = jnp.einsum('nbqhd,nbkhd->nbhqk', qb, kb, preferred_element_type=jnp.float32) * (hd ** -0.5)
    s = s - slopes[:, None, None] * dist[:, None]
    s = jnp.where(valid[:, None], s, NEG_INF)
    m = jnp.max(s, axis=-1, keepdims=True)
    p = jnp.exp(s - m)
    l = jnp.sum(p, axis=-1, keepdims=True)
    o = jnp.einsum('nbhqk,nbkhd->nbqhd', (p / l).astype(v.dtype), vb)
    lse = (m + jnp.log(l))[..., 0].transpose(0, 1, 3, 2)
    o = o.reshape(nbat, n_pad, nh, hd)[:, :n]
    lse = lse.reshape(nbat, n_pad, nh)[:, :n]
    return from_residues(o), from_residues(lse)


def mixer_dilated(qkv):
    bsz, seq, _ = qkv.shape
    ng = len(A_GROUPS)
    qkv = qkv.reshape(bsz, seq, 3, ng, A_HEADS, HEAD_DIM)
    slopes = alibi_slopes(ng * A_HEADS).reshape(ng, A_HEADS)
    outs, lses = [], []
    for gi, (window, dil) in enumerate(A_GROUPS):
        o, lse = dilated_window_attention(qkv[:, :, 0, gi], qkv[:, :, 1, gi], qkv[:, :, 2, gi],
                                          dil, window // (2 * dil), slopes[gi])
        outs.append(o.astype(jnp.float32))
        lses.append(lse)
    wts = jax.nn.softmax(jnp.stack(lses), axis=0)
    o = jnp.sum(wts[..., None] * jnp.stack(outs), axis=0)
    return o.reshape(bsz, seq, A_HEADS * HEAD_DIM).astype(qkv.dtype)


def neighbourhood_attention(q, k, v, rpb):
    bsz, seq, nh, hd = q.shape
    rows = seq // GRID_W
    kr = min(NA_KR, rows)
    n_cb = GRID_W // NA_QCB
    qc = np.arange(GRID_W).reshape(n_cb, NA_QCB)
    slab0 = np.clip(np.arange(n_cb) * NA_QCB - NA_KC // 2, 0, GRID_W - NA_KSW)
    kc = slab0[:, None] + np.arange(NA_KSW)[None, :]
    cs = np.clip(qc - NA_KC // 2, 0, GRID_W - NA_KC)
    col_ok = (kc[:, None, :] >= cs[..., None]) & (kc[:, None, :] < cs[..., None] + NA_KC)
    dc_idx = np.clip(kc[:, None, :] - qc[..., None] + NA_KC - 1, 0, 2 * NA_KC - 2)
    qg = q.reshape(bsz, rows, n_cb, NA_QCB, nh, hd).transpose(1, 0, 2, 3, 4, 5)
    kg = k.reshape(bsz, rows, GRID_W, nh, hd)
    vg = v.reshape(bsz, rows, GRID_W, nh, hd)
    rpb = rpb.astype(jnp.float32)
    scale = hd ** -0.5

    def one_row(args):
        r, qr = args
        rs = jnp.clip(r - kr // 2, 0, rows - kr)
        ks = lax.dynamic_slice_in_dim(kg, rs, kr, axis=1)[:, :, kc]
        vs = lax.dynamic_slice_in_dim(vg, rs, kr, axis=1)[:, :, kc]
        dr_idx = rs + jnp.arange(kr) - r + NA_KR - 1
        bias = rpb[:, dr_idx][:, :, dc_idx].transpose(0, 2, 3, 1, 4)
        s = jnp.einsum('bcqhd,bjckhd->bhcqjk', qr, ks, preferred_element_type=jnp.float32) * scale + bias
        s = jnp.where(col_ok[:, :, None, :], s, NEG_INF)
        p = jax.nn.softmax(s.reshape(s.shape[:4] + (kr * NA_KSW,)), axis=-1).reshape(s.shape)
        o = jnp.einsum('bhcqjk,bjckhd->bcqhd', p.astype(vs.dtype), vs)
        return o.reshape(bsz, GRID_W, nh, hd)

    out = lax.map(one_row, (jnp.arange(rows), qg))
    return out.transpose(1, 0, 2, 3, 4).reshape(bsz, seq, nh * hd)


def hyena_filters(seq, w1, b1, w2, b2, w3, freq, log_decay):
    f32 = jnp.float32
    t = jnp.arange(seq, dtype=f32)
    t_norm = t / (seq - 1)
    ang = (2.0 * math.pi / seq) * t
    bands = jnp.linspace(1e-4, HY_BANDS - 1, HY_BANDS, dtype=f32)
    feats = jnp.concatenate([t_norm[:, None], jnp.cos(ang[:, None] * bands), -jnp.sin(ang[:, None] * bands)], axis=-1)
    freq = freq.astype(f32)
    hid = jnp.sin(freq[0] * (feats @ w1.astype(f32) + b1.astype(f32)))
    hid = jnp.sin(freq[1] * (hid @ w2.astype(f32) + b2.astype(f32)))
    raw = (hid @ w3.astype(f32)).reshape(seq, HY_ORDER, 2, HY_CH)
    h = raw * jnp.exp(-t_norm[:, None, None, None] * jnp.exp(log_decay.astype(f32)))
    h = h / (jnp.sum(jnp.abs(h), axis=0, keepdims=True) + 1e-6)
    circ = jnp.concatenate([h[:, :, 0], jnp.zeros((1, HY_ORDER, HY_CH), f32), h[:0:-1, :, 1]], axis=0)
    return jnp.fft.rfft(circ, axis=0)


def long_conv(z, kf, skip):
    seq = z.shape[1]
    zf32 = z.astype(jnp.float32)
    y = jnp.fft.irfft(jnp.fft.rfft(zf32, n=2 * seq, axis=1) * kf[None], n=2 * seq, axis=1)[:, :seq]
    return y + zf32 * skip.astype(jnp.float32)


def mixer_hyena(u, conv_w, conv_b, hy_w1, hy_b1, hy_w2, hy_b2, hy_w3, hy_freq, hy_log_decay, hy_skip):
    seq = u.shape[1]
    u = lax.conv_general_dilated(u, conv_w[:, None, :].astype(u.dtype), window_strides=(1,), padding='SAME',
                                 dimension_numbers=('NWC', 'WIO', 'NWC'),
                                 feature_group_count=u.shape[-1]) + conv_b
    v, x1, x2 = jnp.split(u, 3, axis=-1)
    kf = hyena_filters(seq, hy_w1, hy_b1, hy_w2, hy_b2, hy_w3, hy_freq, hy_log_decay)
    z = x1.astype(jnp.float32) * long_conv(v, kf[:, 0], hy_skip[0])
    z = x2.astype(jnp.float32) * long_conv(z, kf[:, 1], hy_skip[1])
    return z.astype(u.dtype)


def encoder_layer(x, c, norm_g, w_ada, b_ada, w_in, w_branch, w_out, na_rpb, conv_w, conv_b,
                  hy_w1, hy_b1, hy_w2, hy_b2, hy_w3, hy_freq, hy_log_decay, hy_skip):
    bsz, seq, dm = x.shape
    mod = (jax.nn.silu(c) @ w_ada + b_ada)[:, None, :]
    shift, scale, gate = jnp.split(mod, 3, axis=-1)
    h = rms_norm(x, norm_g) * (1 + scale) + shift
    proj = h @ w_in
    y_a = mixer_dilated(proj[..., :OFF_AZ]) * jax.nn.silu(proj[..., OFF_AZ:OFF_B])
    qkv_b = proj[..., OFF_B:OFF_BZ].reshape(bsz, seq, 3, B_HEADS, HEAD_DIM)
    y_b = neighbourhood_attention(qkv_b[:, :, 0], qkv_b[:, :, 1], qkv_b[:, :, 2], na_rpb) \
        * jax.nn.silu(proj[..., OFF_BZ:OFF_C])
    y_c = mixer_hyena(proj[..., OFF_C:OFF_CZ], conv_w, conv_b, hy_w1, hy_b1, hy_w2, hy_b2, hy_w3,
                      hy_freq, hy_log_decay, hy_skip) * jax.nn.silu(proj[..., OFF_CZ:OFF_G])
    merge = jax.nn.sigmoid(proj[..., OFF_G:].reshape(bsz, seq, N_BRANCH, dm))
    mixed = merge[:, :, 0] * (y_a @ w_branch[0])
    mixed = mixed + merge[:, :, 1] * (y_b @ w_branch[1])
    mixed = mixed + merge[:, :, 2] * (y_c @ w_branch[2])
    return x + gate * (mixed @ w_out)


def trunk(x, c, norm_g, w_ada, b_ada, w_in, w_branch, w_out, na_rpb, conv_w, conv_b,
          hy_w1, hy_b1, hy_w2, hy_b2, hy_w3, hy_freq, hy_log_decay, hy_skip, final_g):
    for l in range(DEPTH):
        x = encoder_layer(x, c, norm_g[l], w_ada[l], b_ada[l], w_in[l], w_branch[l], w_out[l], na_rpb[l],
                          conv_w[l], conv_b[l], hy_w1[l], hy_b1[l], hy_w2[l], hy_b2[l], hy_w3[l],
                          hy_freq[l], hy_log_decay[l], hy_skip[l])
    return rms_norm(x, final_g)


def setup_inputs(seed: int = 0) -> dict:
    key = jax.random.key(seed)
    ks = jax.random.split(key, 24)

    def nrm(k, shape, s):
        return jax.random.normal(k, shape, jnp.float32) * s

    return {
        "x_prompt": nrm(ks[0], (BATCH, SEQ, D_MODEL), 1.0),
        "x_sample": nrm(ks[1], (DEC_BATCH, DEC_SEQ, D_MODEL), 1.0),
        "c_prompt": nrm(ks[2], (BATCH, D_MODEL), 1.0),
        "c_sample": nrm(ks[3], (DEC_BATCH, D_MODEL), 1.0),
        "norm_g": 1.0 + nrm(ks[4], (DEPTH, D_MODEL), 0.02),
        "w_ada": nrm(ks[5], (DEPTH, D_MODEL, 3 * D_MODEL), 0.5 * D_MODEL ** -0.5),
        "b_ada": nrm(ks[6], (DEPTH, 3 * D_MODEL), 0.02),
        "w_in": nrm(ks[7], (DEPTH, D_MODEL, IN_COLS), D_MODEL ** -0.5),
        "w_branch": nrm(ks[8], (DEPTH, N_BRANCH, BRANCH_W, D_MODEL), BRANCH_W ** -0.5),
        "w_out": nrm(ks[9], (DEPTH, D_MODEL, D_MODEL), D_MODEL ** -0.5),
        "na_rpb": nrm(ks[10], (DEPTH, B_HEADS, 2 * NA_KR - 1, 2 * NA_KC - 1), 0.1),
        "conv_w": nrm(ks[11], (DEPTH, HY_CONV, C_PROJ), HY_CONV ** -0.5),
        "conv_b": nrm(ks[12], (DEPTH, C_PROJ), 0.02),
        "hy_w1": nrm(ks[13], (DEPTH, HY_EMB, HY_FFN), HY_EMB ** -0.5),
        "hy_b1": nrm(ks[14], (DEPTH, HY_FFN), 0.02),
        "hy_w2": nrm(ks[15], (DEPTH, HY_FFN, HY_FFN), HY_FFN ** -0.5),
        "hy_b2": nrm(ks[16], (DEPTH, HY_FFN), 0.02),
        "hy_w3": nrm(ks[17], (DEPTH, HY_FFN, HY_ORDER * 2 * HY_CH), HY_FFN ** -0.5),
        "hy_freq": 1.0 + nrm(ks[18], (DEPTH, 2, HY_FFN), 0.1),
        "hy_log_decay": jax.random.uniform(ks[19], (DEPTH, HY_ORDER, 2, HY_CH), jnp.float32,
                                           minval=math.log(3.0), maxval=math.log(15.0)),
        "hy_skip": nrm(ks[20], (DEPTH, HY_ORDER, HY_CH), 0.5),
        "final_g": 1.0 + nrm(ks[21], (D_MODEL,), 0.02),
    }


def reference(x_prompt, x_sample, c_prompt, c_sample, norm_g, w_ada, b_ada, w_in, w_branch, w_out,
              na_rpb, conv_w, conv_b, hy_w1, hy_b1, hy_w2, hy_b2, hy_w3, hy_freq, hy_log_decay,
              hy_skip, final_g):
    y_prompt = trunk(x_prompt, c_prompt, norm_g, w_ada, b_ada, w_in, w_branch, w_out, na_rpb, conv_w,
                     conv_b, hy_w1, hy_b1, hy_w2, hy_b2, hy_w3, hy_freq, hy_log_decay, hy_skip, final_g)
    y_sample = trunk(x_sample, c_sample, norm_g, w_ada, b_ada, w_in, w_branch, w_out, na_rpb, conv_w,
                     conv_b, hy_w1, hy_b1, hy_w2, hy_b2, hy_w3, hy_freq, hy_log_decay, hy_skip, final_g)
    return (y_prompt, y_sample)
```

```python
import functools
import math

import jax
import jax.numpy as jnp
import numpy as np
from jax import lax
from jax.experimental import pallas as pl
from jax.experimental.pallas import tpu as pltpu

F32 = jnp.float32
BF16 = jnp.bfloat16

LANES = 128
V7X_VMEM_LIMIT = 56 * 1024 * 1024

HEAD_DIM = 128
GRID_W = 64
BRANCH_W = 1024
N_BRANCH = 3
A_GROUPS = ((128, 1), (512, 4), (2048, 16))
A_HEADS = 8
A_RADIUS = 64
B_HEADS = 8
NA_KR = 8
NA_KC = 16
HY_ORDER = 2
HY_CH = BRANCH_W
HY_BANDS = 16
HY_EMB = 1 + 2 * HY_BANDS
HY_FFN = 64
RMS_EPS = 1e-6
NEG_INF = -1e30

A_QKV = 3 * len(A_GROUPS) * A_HEADS * HEAD_DIM
B_QKV = 3 * B_HEADS * HEAD_DIM
C_PROJ = (HY_ORDER + 1) * HY_CH
OFF_AZ = A_QKV
OFF_B = OFF_AZ + BRANCH_W
OFF_BZ = OFF_B + B_QKV
OFF_C = OFF_BZ + BRANCH_W
OFF_CZ = OFF_C + C_PROJ
OFF_G = OFF_CZ + BRANCH_W

NA_QROWS = 4
NA_KROWS = 12
NA_QT = NA_QROWS * GRID_W
NA_KT = NA_KROWS * GRID_W

A_QB = 128


def _cparams(sem):
    return pltpu.CompilerParams(dimension_semantics=sem, vmem_limit_bytes=V7X_VMEM_LIMIT)


def _tile(n, pref):
    t = min(n, pref)
    while n % t:
        t //= 2
    return t


def _silu(z):
    return z * (1.0 / (1.0 + jnp.exp(-z)))


def _sigmoid(z):
    return 1.0 / (1.0 + jnp.exp(-z))


def _ada_kernel(c_ref, w_ref, b_ref, o_ref):
    a = _silu(c_ref[...]).astype(BF16)
    o_ref[...] = jnp.dot(a, w_ref[...].astype(BF16), preferred_element_type=F32) + b_ref[...]


def ada_mod(c, w_ada, b_ada):
    nb, d = c.shape
    n = w_ada.shape[1]
    tn = _tile(n, 512)
    return pl.pallas_call(
        _ada_kernel,
        grid=(n // tn,),
        in_specs=[pl.BlockSpec((nb, d), lambda j: (0, 0)),
                  pl.BlockSpec((d, tn), lambda j: (0, j)),
                  pl.BlockSpec((1, tn), lambda j: (0, j))],
        out_specs=pl.BlockSpec((nb, tn), lambda j: (0, j)),
        out_shape=jax.ShapeDtypeStruct((nb, n), F32),
        compiler_params=_cparams(("arbitrary",)),
        name="ada_mod",
    )(c, w_ada, b_ada.reshape(1, n))


def _norm_mod_kernel(x_ref, g_ref, mod_ref, o_ref):
    x = x_ref[0]
    y = x * lax.rsqrt(jnp.mean(x * x, axis=-1, keepdims=True) + RMS_EPS) * g_ref[...]
    shift = mod_ref[0, 0:1, :]
    scale = mod_ref[0, 1:2, :]
    o_ref[0] = (y * (1.0 + scale) + shift).astype(o_ref.dtype)


def norm_mod(x, g, mod3):
    b, l, d = x.shape
    tm = _tile(l, 256)
    return pl.pallas_call(
        _norm_mod_kernel,
        grid=(b, l // tm),
        in_specs=[pl.BlockSpec((1, tm, d), lambda i, t: (i, t, 0)),
                  pl.BlockSpec((1, d), lambda i, t: (0, 0)),
                  pl.BlockSpec((1, 3, d), lambda i, t: (i, 0, 0))],
        out_specs=pl.BlockSpec((1, tm, d), lambda i, t: (i, t, 0)),
        out_shape=jax.ShapeDtypeStruct((b, l, d), BF16),
        compiler_params=_cparams(("parallel", "parallel")),
        name="norm_mod",
    )(x, g.reshape(1, d), mod3)


def _final_norm_kernel(x_ref, g_ref, o_ref):
    x = x_ref[0]
    o_ref[0] = x * lax.rsqrt(jnp.mean(x * x, axis=-1, keepdims=True) + RMS_EPS) * g_ref[...]


def final_norm(x, g):
    b, l, d = x.shape
    tm = _tile(l, 256)
    return pl.pallas_call(
        _final_norm_kernel,
        grid=(b, l // tm),
        in_specs=[pl.BlockSpec((1, tm, d), lambda i, t: (i, t, 0)),
                  pl.BlockSpec((1, d), lambda i, t: (0, 0))],
        out_specs=pl.BlockSpec((1, tm, d), lambda i, t: (i, t, 0)),
        out_shape=jax.ShapeDtypeStruct((b, l, d), F32),
        compiler_params=_cparams(("parallel", "parallel")),
        name="final_norm",
    )(x, g.reshape(1, d))


def _mm_kernel(a_ref, b_ref, o_ref):
    o_ref[...] = jnp.dot(a_ref[...], b_ref[...], preferred_element_type=F32).astype(o_ref.dtype)


def matmul(a, w, out_dtype=BF16, tm_pref=1024, tn_pref=512):
    m, k = a.shape
    n = w.shape[1]
    tm, tn = _tile(m, tm_pref), _tile(n, tn_pref)
    return pl.pallas_call(
        _mm_kernel,
        grid=(m // tm, n // tn),
        in_specs=[pl.BlockSpec((tm, k), lambda i, j: (i, 0)),
                  pl.BlockSpec((k, tn), lambda i, j: (0, j))],
        out_specs=pl.BlockSpec((tm, tn), lambda i, j: (i, j)),
        out_shape=jax.ShapeDtypeStruct((m, n), out_dtype),
        compiler_params=_cparams(("parallel", "arbitrary")),
        name="in_proj",
    )(a, w)


def _attn_a_kernel(slopes_ref, *refs, seq):
    qkv_refs = refs[:9]
    z_ref = refs[9]
    o_ref = refs[10]
    stage, qd, kd, vd, o_sc, lse_sc = refs[11:]
    h = pl.program_id(1)
    scale = HEAD_DIM ** -0.5

    for g, (_, dil) in enumerate(A_GROUPS):
        n = seq // dil
        qb = min(A_QB, n)
        slab = min(qb + 2 * A_RADIUS, n)
        nqb = n // qb
        q_ref, k_ref, v_ref = qkv_refs[g], qkv_refs[3 + g], qkv_refs[6 + g]
        if dil == 1:
            q_src, k_src, v_src = q_ref.at[0], k_ref.at[0], v_ref.at[0]
        else:
            for src, dst in ((q_ref, qd), (k_ref, kd), (v_ref, vd)):
                stage[...] = src[0].astype(F32)
                for r in range(dil):
                    dst[pl.ds(r * n, n), :] = stage[pl.ds(r, n, stride=dil), :].astype(BF16)
            q_src, k_src, v_src = qd, kd, vd
        slope = slopes_ref[g * A_HEADS + h] * float(dil)

        def block(blk, carry, g=g, dil=dil, n=n, qb=qb, slab=slab, nqb=nqb,
                  q_src=q_src, k_src=k_src, v_src=v_src, slope=slope):
            r = blk // nqb
            q0 = (blk % nqb) * qb
            ks = jnp.clip(q0 - A_RADIUS, 0, n - slab)
            base = r * n
            q = q_src[pl.ds(pl.multiple_of(base + q0, 64), qb), :]
            k = k_src[pl.ds(pl.multiple_of(base + ks, 64), slab), :]
            v = v_src[pl.ds(pl.multiple_of(base + ks, 64), slab), :]
            s = lax.dot_general(q, k, (((1,), (1,)), ((), ())), preferred_element_type=F32) * scale
            rel = (lax.broadcasted_iota(jnp.int32, (qb, slab), 1)
                   - lax.broadcasted_iota(jnp.int32, (qb, slab), 0) + (ks - q0))
            dist = jnp.abs(rel)
            s = jnp.where(dist <= A_RADIUS, s - slope * dist.astype(F32), NEG_INF)
            m = jnp.max(s, axis=-1, keepdims=True)
            p = jnp.exp(s - m)
            l = jnp.sum(p, axis=-1, keepdims=True)
            o = jnp.dot(p.astype(BF16), v, preferred_element_type=F32) * (1.0 / l)
            lse = jnp.broadcast_to(m + jnp.log(l), (qb, LANES))
            if dil == 1:
                rows = pl.ds(pl.multiple_of(q0, 64), qb)
            else:
                rows = pl.ds(q0 * dil + r, qb, stride=dil)
            o_sc[g, rows, :] = o
            lse_sc[g, rows, :] = lse
            return carry

        lax.fori_loop(0, dil * nqb, block, 0)

    ch = min(256, seq)

    def merge(c, carry):
        rows = pl.ds(pl.multiple_of(c * ch, ch), ch)
        l0, l1, l2 = lse_sc[0, rows, :], lse_sc[1, rows, :], lse_sc[2, rows, :]
        mx = jnp.maximum(jnp.maximum(l0, l1), l2)
        e0, e1, e2 = jnp.exp(l0 - mx), jnp.exp(l1 - mx), jnp.exp(l2 - mx)
        y = (e0 * o_sc[0, rows, :] + e1 * o_sc[1, rows, :] + e2 * o_sc[2, rows, :]) * (1.0 / (e0 + e1 + e2))
        o_ref[0, rows, :] = (y * _silu(z_ref[0, rows, :].astype(F32))).astype(o_ref.dtype)
        return carry

    lax.fori_loop(0, seq // ch, merge, 0)


def attn_a(proj, slopes):
    b, l, _ = proj.shape
    ng = len(A_GROUPS)

    def qkv_spec(s, g):
        return pl.BlockSpec((1, l, HEAD_DIM), lambda i, h, s=s, g=g: (i, 0, (s * ng + g) * A_HEADS + h))

    in_specs = [pl.BlockSpec(memory_space=pltpu.SMEM)]
    in_specs += [qkv_spec(s, g) for s in range(3) for g in range(ng)]
    in_specs += [pl.BlockSpec((1, l, HEAD_DIM), lambda i, h: (i, 0, OFF_AZ // HEAD_DIM + h))]
    return pl.pallas_call(
        functools.partial(_attn_a_kernel, seq=l),
        grid=(b, A_HEADS),
        in_specs=in_specs,
        out_specs=pl.BlockSpec((1, l, HEAD_DIM), lambda i, h: (i, 0, h)),
        out_shape=jax.ShapeDtypeStruct((b, l, BRANCH_W), BF16),
        scratch_shapes=[pltpu.VMEM((l, LANES), F32),
                        pltpu.VMEM((l, LANES), BF16), pltpu.VMEM((l, LANES), BF16),
                        pltpu.VMEM((l, LANES), BF16),
                        pltpu.VMEM((ng, l, LANES), F32), pltpu.VMEM((ng, l, LANES), F32)],
        compiler_params=_cparams(("parallel", "parallel")),
        name="attn_dilated",
    )(slopes, *([proj] * 10))


def _na_bias_tables(rpb, rows):
    kr = min(NA_KR, rows)
    tabs = []
    for off in (0, NA_QROWS, 2 * NA_QROWS):
        a = {0: 0, NA_QROWS: 1, 2 * NA_QROWS: rows // NA_QROWS - 1}[off]
        ks = NA_QROWS * a - off
        rq = NA_QROWS * a + np.arange(NA_QROWS)
        rs = np.clip(rq - kr // 2, 0, rows - kr)
        rk = ks + np.arange(NA_KROWS)
        row_ok = (rk[None, :] >= rs[:, None]) & (rk[None, :] < rs[:, None] + kr)
        dr = np.clip(rk[None, :] - rq[:, None] + NA_KR - 1, 0, 2 * NA_KR - 2)
        qc = np.arange(GRID_W)
        cs = np.clip(qc - NA_KC // 2, 0, GRID_W - NA_KC)
        kc = np.arange(GRID_W)
        col_ok = (kc[None, :] >= cs[:, None]) & (kc[None, :] < cs[:, None] + NA_KC)
        dc = np.clip(kc[None, :] - qc[:, None] + NA_KC - 1, 0, 2 * NA_KC - 2)
        shape = (NA_QROWS, GRID_W, NA_KROWS, GRID_W)
        dr4 = np.broadcast_to(dr[:, None, :, None], shape).reshape(NA_QT, NA_KT)
        dc4 = np.broadcast_to(dc[None, :, None, :], shape).reshape(NA_QT, NA_KT)
        ok4 = np.broadcast_to(row_ok[:, None, :, None] & col_ok[None, :, None, :], shape).reshape(NA_QT, NA_KT)
        tabs.append(jnp.where(ok4[None], rpb[:, dr4, dc4], NEG_INF))
    return jnp.stack(tabs, axis=1).astype(F32)


def _attn_b_kernel(q_ref, k_ref, v_ref, z_ref, bias_ref, o_ref, *, seq):
    rows = seq // GRID_W
    nblk = rows // NA_QROWS
    scale = HEAD_DIM ** -0.5

    def block(a, carry):
        ks_row = jnp.clip(NA_QROWS * a - NA_QROWS, 0, rows - NA_KROWS)
        cls = (NA_QROWS * a - ks_row) // NA_QROWS
        qrows = pl.ds(pl.multiple_of(a * NA_QT, NA_QT), NA_QT)
        krows = pl.ds(pl.multiple_of(ks_row * GRID_W, GRID_W), NA_KT)
        q = q_ref[0, qrows, :]
        k = k_ref[0, krows, :]
        v = v_ref[0, krows, :]
        s = lax.dot_general(q, k, (((1,), (1,)), ((), ())), preferred_element_type=F32) * scale
        s = s + bias_ref[0, cls]
        m = jnp.max(s, axis=-1, keepdims=True)
        p = jnp.exp(s - m)
        l = jnp.sum(p, axis=-1, keepdims=True)
        o = jnp.dot(p.astype(BF16), v, preferred_element_type=F32) * (1.0 / l)
        o_ref[0, qrows, :] = (o * _silu(z_ref[0, qrows, :].astype(F32))).astype(o_ref.dtype)
        return carry

    lax.fori_loop(0, nblk, block, 0)


def attn_b(proj, bias):
    b, l, _ = proj.shape
    assert l % NA_QT == 0 and l // GRID_W >= NA_KROWS

    def col_spec(off):
        return pl.BlockSpec((1, l, HEAD_DIM), lambda h, i, off=off: (i, 0, off // HEAD_DIM + h))

    return pl.pallas_call(
        functools.partial(_attn_b_kernel, seq=l),
        grid=(B_HEADS, b),
        in_specs=[col_spec(OFF_B), col_spec(OFF_B + B_HEADS * HEAD_DIM),
                  col_spec(OFF_B + 2 * B_HEADS * HEAD_DIM), col_spec(OFF_BZ),
                  pl.BlockSpec((1, 3, NA_QT, NA_KT), lambda h, i: (h, 0, 0, 0))],
        out_specs=pl.BlockSpec((1, l, HEAD_DIM), lambda h, i: (i, 0, h)),
        out_shape=jax.ShapeDtypeStruct((b, l, BRANCH_W), BF16),
        compiler_params=_cparams(("parallel", "parallel")),
        name="attn_neighbourhood",
    )(proj, proj, proj, proj, bias)


def _conv3_kernel(u_ref, w_ref, b_ref, o_ref, *, seq):
    x = u_ref[0].astype(F32)
    t = lax.broadcasted_iota(jnp.int32, x.shape, 0)
    prev = jnp.where(t == 0, 0.0, pltpu.roll(x, 1, axis=0))
    nxt = jnp.where(t == seq - 1, 0.0, pltpu.roll(x, seq - 1, axis=0))
    y = w_ref[0:1, :] * prev + w_ref[1:2, :] * x + w_ref[2:3, :] * nxt + b_ref[...]
    o_ref[0] = y.astype(o_ref.dtype)


def conv3(proj, conv_w, conv_b):
    b, l, _ = proj.shape
    tc = 256
    return pl.pallas_call(
        functools.partial(_conv3_kernel, seq=l),
        grid=(b, C_PROJ // tc),
        in_specs=[pl.BlockSpec((1, l, tc), lambda i, j: (i, 0, OFF_C // tc + j)),
                  pl.BlockSpec((3, tc), lambda i, j: (0, j)),
                  pl.BlockSpec((1, tc), lambda i, j: (0, j))],
        out_specs=pl.BlockSpec((1, l, tc), lambda i, j: (i, 0, j)),
        out_shape=jax.ShapeDtypeStruct((b, l, C_PROJ), BF16),
        compiler_params=_cparams(("parallel", "parallel")),
        name="hyena_conv3",
    )(proj, conv_w, conv_b.reshape(1, C_PROJ))


def _filter_kernel(feats_ref, w1_ref, b1_ref, w2_ref, b2_ref, freq_ref, w3_ref, dec_ref, o_ref, hid_ref,
                   *, seq, tc):
    j = pl.program_id(0)
    hp = lax.Precision.HIGHEST

    @pl.when(j == 0)
    def _():
        h1 = jnp.sin(freq_ref[0:1, :] * (jnp.dot(feats_ref[...], w1_ref[...], precision=hp,
                                                 preferred_element_type=F32) + b1_ref[...]))
        hid_ref[...] = jnp.sin(freq_ref[1:2, :] * (jnp.dot(h1, w2_ref[...], precision=hp,
                                                           preferred_element_type=F32) + b2_ref[...]))

    raw = jnp.dot(hid_ref[...], w3_ref[...], precision=hp, preferred_element_type=F32)
    t_norm = feats_ref[:, 0:1]
    hfil = raw * jnp.exp(-t_norm * jnp.exp(dec_ref[...]))
    hfil = hfil * (1.0 / (jnp.sum(jnp.abs(hfil), axis=0, keepdims=True) + 1e-6))
    dropped_row = ((j * tc) // HY_CH) % 2 - 1
    t = lax.broadcasted_iota(jnp.int32, hfil.shape, 0)
    o_ref[...] = jnp.where(t == dropped_row, 0.0, hfil).astype(o_ref.dtype)


def hyena_filter_taps(seq, w1, b1, w2, b2, w3, freq, log_decay):
    t = jnp.arange(seq, dtype=F32)
    t_norm = t / (seq - 1)
    ang = (2.0 * math.pi / seq) * t
    bands = jnp.linspace(1e-4, HY_BANDS - 1, HY_BANDS, dtype=F32)
    feats = jnp.concatenate([t_norm[:, None], jnp.cos(ang[:, None] * bands), -jnp.sin(ang[:, None] * bands)], axis=-1)
    emb = 64
    feats = jnp.pad(feats, ((0, 0), (0, emb - HY_EMB)))
    w1p = jnp.pad(w1, ((0, emb - HY_EMB), (0, 0)))
    ncol = HY_ORDER * 2 * HY_CH
    tc = 256
    full = lambda shape: pl.BlockSpec(shape, lambda j: (0, 0))
    return pl.pallas_call(
        functools.partial(_filter_kernel, seq=seq, tc=tc),
        grid=(ncol // tc,),
        in_specs=[full((seq, emb)), full((emb, HY_FFN)), full((1, HY_FFN)), full((HY_FFN, HY_FFN)),
                  full((1, HY_FFN)), full((2, HY_FFN)),
                  pl.BlockSpec((HY_FFN, tc), lambda j: (0, j)),
                  pl.BlockSpec((1, tc), lambda j: (0, j))],
        out_specs=pl.BlockSpec((seq, tc), lambda j: (0, j)),
        out_shape=jax.ShapeDtypeStruct((seq, ncol), BF16),
        scratch_shapes=[pltpu.VMEM((seq, HY_FFN), F32)],
        compiler_params=_cparams(("arbitrary",)),
        name="hyena_filter_taps",
    )(feats, w1p, b1.reshape(1, HY_FFN), w2, b2.reshape(1, HY_FFN), freq, w3, log_decay.reshape(1, ncol))


def _dft_mats(seq):
    n = 2 * seq
    k = jnp.arange(seq, dtype=jnp.int32)
    ang = ((k[:, None] * k[None, :]) % n).astype(F32) * (2.0 * math.pi / n)
    cmat = jnp.cos(ang)
    smat = jnp.sin(ang)
    sign = (1 - 2 * (k % 2)).astype(F32)
    smat = jnp.where((k == 0)[:, None], sign[None, :], smat)
    return cmat.astype(BF16), smat.astype(BF16), smat.T.astype(BF16)


def _filter_spec_kernel(c_ref, s_ref, xf_ref, xb_ref, kra_ref, kia_ref, krb_ref, *, seq, tf):
    i = pl.program_id(2)
    cm, sm = c_ref[...], s_ref[...]
    dot = lambda a, b: jnp.dot(a, b, preferred_element_type=F32)
    xf, xb = xf_ref[...], xb_ref[...]
    kr = dot(cm, xf) + dot(cm, xb)
    sq_f, sq_b = dot(sm, xf), dot(sm, xb)
    ki = sq_b - sq_f
    nyq = sq_f + sq_b
    row = lax.broadcasted_iota(jnp.int32, kr.shape, 0) + i * tf
    first = row == 0
    inv_n = 1.0 / (2 * seq)
    w = jnp.where(first, inv_n, 2.0 * inv_n)
    kra_ref[0] = kr * w
    kia_ref[0] = jnp.where(first, 0.0, ki * w)
    krb_ref[0] = jnp.where(first, nyq * inv_n, kr * w)


def hyena_filter_spectrum(taps, cmat, smat):
    seq = taps.shape[0]
    tf, tc = _tile(seq, 256), 512
    ncb = HY_CH // tc
    out = jax.ShapeDtypeStruct((HY_ORDER, seq, HY_CH), F32)
    ospec = pl.BlockSpec((1, tf, tc), lambda o, j, i: (o, i, j))
    return pl.pallas_call(
        functools.partial(_filter_spec_kernel, seq=seq, tf=tf),
        grid=(HY_ORDER, ncb, seq // tf),
        in_specs=[pl.BlockSpec((tf, seq), lambda o, j, i: (i, 0)),
                  pl.BlockSpec((tf, seq), lambda o, j, i: (i, 0)),
                  pl.BlockSpec((seq, tc), lambda o, j, i: (0, (o * 2) * ncb + j)),
                  pl.BlockSpec((seq, tc), lambda o, j, i: (0, (o * 2 + 1) * ncb + j))],
        out_specs=[ospec, ospec, ospec],
        out_shape=[out, out, out],
        compiler_params=_cparams(("parallel", "parallel", "arbitrary")),
        name="hyena_filter_spectrum",
    )(cmat, smat, taps, taps)


def _conv_fwd_kernel(c_ref, s_ref, z_ref, kra_ref, kia_ref, krb_ref, yr_ref, qy_ref):
    z = z_ref[0]
    p = jnp.dot(c_ref[...], z, preferred_element_type=F32)
    q = jnp.dot(s_ref[...], z, preferred_element_type=F32)
    kia = kia_ref[0]
    yr_ref[0] = (p * kra_ref[0] + q * kia).astype(yr_ref.dtype)
    qy_ref[0] = (q * krb_ref[0] - p * kia).astype(qy_ref.dtype)


def conv_fwd(z, zcol, cmat, smat, tabs, order):
    b, seq, _ = z.shape
    tf, tc = _tile(seq, 256), 512
    ncb = HY_CH // tc
    out = jax.ShapeDtypeStruct((b, seq, HY_CH), BF16)
    ospec = pl.BlockSpec((1, tf, tc), lambda n, j, i: (n, i, j))
    kspec = pl.BlockSpec((1, tf, tc), lambda n, j, i: (order, i, j))
    return pl.pallas_call(
        _conv_fwd_kernel,
        grid=(b, ncb, seq // tf),
        in_specs=[pl.BlockSpec((tf, seq), lambda n, j, i: (i, 0)),
                  pl.BlockSpec((tf, seq), lambda n, j, i: (i, 0)),
                  pl.BlockSpec((1, seq, tc), lambda n, j, i: (n, 0, zcol // tc + j)),
                  kspec, kspec, kspec],
        out_specs=[ospec, ospec],
        out_shape=[out, out],
        compiler_params=_cparams(("parallel", "parallel", "arbitrary")),
        name="hyena_conv_fwd",
    )(cmat, smat, z, *tabs)


def _conv_inv_kernel(c_ref, st_ref, yr_ref, qy_ref, x_ref, zp_ref, skip_ref, *rest, gated):
    o_ref = rest[-1]
    y = (jnp.dot(c_ref[...], yr_ref[0], preferred_element_type=F32)
         + jnp.dot(st_ref[...], qy_ref[0], preferred_element_type=F32))
    y = x_ref[0].astype(F32) * (y + zp_ref[0].astype(F32) * skip_ref[0])
    if gated:
        y = y * _silu(rest[0][0].astype(F32))
    o_ref[0] = y.astype(o_ref.dtype)


def conv_inv(yr, qy, cmat, smat_t, xmul, xcol, zprev, zcol, skip, gate=None, gcol=0):
    b, seq, _ = yr.shape
    tt, tc = _tile(seq, 256), 512
    ncb = HY_CH // tc
    yspec = pl.BlockSpec((1, seq, tc), lambda n, j, i: (n, 0, j))
    mspec = pl.BlockSpec((tt, seq), lambda n, j, i: (i, 0))

    def tile_spec(col):
        return pl.BlockSpec((1, tt, tc), lambda n, j, i, col=col: (n, i, col // tc + j))

    in_specs = [mspec, mspec, yspec, yspec, tile_spec(xcol), tile_spec(zcol),
                pl.BlockSpec((1, 1, tc), lambda n, j, i: (0, 0, j))]
    args = [cmat, smat_t, yr, qy, xmul, zprev, skip.reshape(1, 1, HY_CH)]
    if gate is not None:
        in_specs.append(tile_spec(gcol))
        args.append(gate)
    return pl.pallas_call(
        functools.partial(_conv_inv_kernel, gated=gate is not None),
        grid=(b, ncb, seq // tt),
        in_specs=in_specs,
        out_specs=pl.BlockSpec((1, tt, tc), lambda n, j, i: (n, i, j)),
        out_shape=jax.ShapeDtypeStruct((b, seq, HY_CH), BF16),
        compiler_params=_cparams(("parallel", "parallel", "arbitrary")),
        name="hyena_conv_inv",
    )(*args)


def mixer_hyena(proj, cmat, smat, smat_t, tabs, conv_w, conv_b, skip):
    u = conv3(proj, conv_w, conv_b)
    yr, qy = conv_fwd(u, 0, cmat, smat, tabs, 0)
    z1 = conv_inv(yr, qy, cmat, smat_t, u, HY_CH, u, 0, skip[0])
    yr, qy = conv_fwd(z1, 0, cmat, smat, tabs, 1)
    return conv_inv(yr, qy, cmat, smat_t, u, 2 * HY_CH, z1, 0, skip[1], gate=proj, gcol=OFF_CZ)


def _merge_kernel(ya_ref, yb_ref, yc_ref, w_ref, g0_ref, g1_ref, g2_ref, o_ref):
    acc = None
    for y_ref, g_ref, n in ((ya_ref, g0_ref, 0), (yb_ref, g1_ref, 1), (yc_ref, g2_ref, 2)):
        t = _sigmoid(g_ref[...].astype(F32)) * jnp.dot(y_ref[...], w_ref[n], preferred_element_type=F32)
        acc = t if acc is None else acc + t
    o_ref[...] = acc.astype(o_ref.dtype)


def merge_branches(ya, yb, yc, w_branch, proj2d, d):
    m = ya.shape[0]
    tm, tn = _tile(m, 1024), _tile(d, 512)
    yspec = pl.BlockSpec((tm, BRANCH_W), lambda i, j: (i, 0))

    def gspec(n):
        return pl.BlockSpec((tm, tn), lambda i, j, n=n: (i, (OFF_G + n * d) // tn + j))

    return pl.pallas_call(
        _merge_kernel,
        grid=(m // tm, d // tn),
        in_specs=[yspec, yspec, yspec,
                  pl.BlockSpec((N_BRANCH, BRANCH_W, tn), lambda i, j: (0, 0, j)),
                  gspec(0), gspec(1), gspec(2)],
        out_specs=pl.BlockSpec((tm, tn), lambda i, j: (i, j)),
        out_shape=jax.ShapeDtypeStruct((m, d), BF16),
        compiler_params=_cparams(("parallel", "arbitrary")),
        name="branch_merge",
    )(ya, yb, yc, w_branch, proj2d, proj2d, proj2d)


def _out_kernel(a_ref, w_ref, x_ref, mod_ref, o_ref):
    acc = jnp.dot(a_ref[0], w_ref[...], preferred_element_type=F32)
    o_ref[0] = x_ref[0] + mod_ref[0, 2:3, :] * acc


def out_proj(mixed, w_out, x, mod3):
    b, l, d = x.shape
    tm, tn = _tile(l, 1024), _tile(d, 512)
    return pl.pallas_call(
        _out_kernel,
        grid=(b, l // tm, d // tn),
        in_specs=[pl.BlockSpec((1, tm, d), lambda n, i, j: (n, i, 0)),
                  pl.BlockSpec((d, tn), lambda n, i, j: (0, j)),
                  pl.BlockSpec((1, tm, tn), lambda n, i, j: (n, i, j)),
                  pl.BlockSpec((1, 3, tn), lambda n, i, j: (n, 0, j))],
        out_specs=pl.BlockSpec((1, tm, tn), lambda n, i, j: (n, i, j)),
        out_shape=jax.ShapeDtypeStruct((b, l, d), F32),
        compiler_params=_cparams(("parallel", "parallel", "arbitrary")),
        name="out_proj",
    )(mixed, w_out, x, mod3)


def _trunk(x, mods, layers, consts, final_g):
    b, l, d = x.shape
    cmat, smat, smat_t = consts["dft"]
    for lw, mod3 in zip(layers, mods):
        h = norm_mod(x, lw["norm_g"], mod3)
        proj2d = matmul(h.reshape(b * l, d), lw["w_in"])
        proj = proj2d.reshape(b, l, -1)
        ya = attn_a(proj, consts["slopes"])
        yb = attn_b(proj, _na_bias_tables(lw["na_rpb"], l // GRID_W))
        taps = hyena_filter_taps(l, lw["hy_w1"], lw["hy_b1"], lw["hy_w2"], lw["hy_b2"], lw["hy_w3"],
                                 lw["hy_freq"], lw["hy_log_decay"])
        tabs = hyena_filter_spectrum(taps, cmat, smat)
        yc = mixer_hyena(proj, cmat, smat, smat_t, tabs, lw["conv_w"], lw["conv_b"], lw["hy_skip"])
        mixed = merge_branches(ya.reshape(b * l, -1), yb.reshape(b * l, -1), yc.reshape(b * l, -1),
                               lw["w_branch"], proj2d, d)
        x = out_proj(mixed.reshape(b, l, d), lw["w_out"], x, mod3)
    return final_norm(x, final_g)


def kernel(x_prompt, x_sample, c_prompt, c_sample, norm_g, w_ada, b_ada, w_in, w_branch, w_out, na_rpb, conv_w, conv_b, hy_w1, hy_b1, hy_w2, hy_b2, hy_w3, hy_freq, hy_log_decay, hy_skip, final_g):
    depth, d = norm_g.shape
    nbp = c_prompt.shape[0]
    c_all = jnp.concatenate([c_prompt, c_sample], axis=0)
    nslope = len(A_GROUPS) * A_HEADS
    slopes = jnp.asarray(2.0 ** (-8.0 * (np.arange(nslope) + 1) / nslope), F32)
    layers, mods_p, mods_s = [], [], []
    for i in range(depth):
        mod3 = ada_mod(c_all, w_ada[i], b_ada[i]).reshape(c_all.shape[0], 3, d)
        mods_p.append(mod3[:nbp])
        mods_s.append(mod3[nbp:])
        layers.append(dict(
            norm_g=norm_g[i], w_in=w_in[i].astype(BF16), w_branch=w_branch[i].astype(BF16),
            w_out=w_out[i].astype(BF16), na_rpb=na_rpb[i], conv_w=conv_w[i], conv_b=conv_b[i],
            hy_w1=hy_w1[i], hy_b1=hy_b1[i], hy_w2=hy_w2[i], hy_b2=hy_b2[i], hy_w3=hy_w3[i],
            hy_freq=hy_freq[i], hy_log_decay=hy_log_decay[i], hy_skip=hy_skip[i]))
    outs = []
    for x, mods in ((x_prompt, mods_p), (x_sample, mods_s)):
        consts = dict(dft=_dft_mats(x.shape[1]), slopes=slopes)
        outs.append(_trunk(x, mods, layers, consts, final_g))
    return tuple(outs)
```

```python
import functools
import math

import jax
import jax.numpy as jnp
import numpy as np
from jax import lax
from jax.experimental import pallas as pl
from jax.experimental.pallas import tpu as pltpu

F32 = jnp.float32
BF16 = jnp.bfloat16

LANES = 128
V7X_VMEM_LIMIT = 56 * 1024 * 1024

HEAD_DIM = 128
GRID_W = 64
BRANCH_W = 1024
N_BRANCH = 3
A_GROUPS = ((128, 1), (512, 4), (2048, 16))
A_HEADS = 8
A_RADIUS = 64
B_HEADS = 8
NA_KR = 8
NA_KC = 16
HY_ORDER = 2
HY_CH = BRANCH_W
HY_BANDS = 16
HY_EMB = 1 + 2 * HY_BANDS
HY_FFN = 64
RMS_EPS = 1e-6
NEG_INF = -1e30

A_QKV = 3 * len(A_GROUPS) * A_HEADS * HEAD_DIM
B_QKV = 3 * B_HEADS * HEAD_DIM
C_PROJ = (HY_ORDER + 1) * HY_CH
OFF_AZ = A_QKV
OFF_B = OFF_AZ + BRANCH_W
OFF_BZ = OFF_B + B_QKV
OFF_C = OFF_BZ + BRANCH_W
OFF_CZ = OFF_C + C_PROJ
OFF_G = OFF_CZ + BRANCH_W

PM_AZ = 3 * BRANCH_W
PM_B = PM_AZ + BRANCH_W
PM_BZ = PM_B + B_QKV
PM_CZ = PM_BZ + BRANCH_W
PM_G = PM_CZ + BRANCH_W
SIDE_C = 3 * (len(A_GROUPS) - 1) * BRANCH_W
SIDE_COLS = SIDE_C + C_PROJ

NA_QROWS = 4
NA_KROWS = 12
NA_QT = NA_QROWS * GRID_W
NA_KT = NA_KROWS * GRID_W

A_QB = 128


def _cparams(sem):
    return pltpu.CompilerParams(dimension_semantics=sem, vmem_limit_bytes=V7X_VMEM_LIMIT)


def _tile(n, pref):
    t = min(n, pref)
    while n % t:
        t //= 2
    return t


def _silu(z):
    return z * (1.0 / (1.0 + jnp.exp(-z)))


def _sigmoid(z):
    return 1.0 / (1.0 + jnp.exp(-z))


def _ada_kernel(c_ref, w_ref, b_ref, o_ref):
    a = _silu(c_ref[...]).astype(BF16)
    o_ref[...] = jnp.dot(a, w_ref[...].astype(BF16), preferred_element_type=F32) + b_ref[...]


def ada_mod(c, w_ada, b_ada):
    nb, d = c.shape
    n = w_ada.shape[1]
    tn = _tile(n, 512)
    return pl.pallas_call(
        _ada_kernel,
        grid=(n // tn,),
        in_specs=[pl.BlockSpec((nb, d), lambda j: (0, 0)),
                  pl.BlockSpec((d, tn), lambda j: (0, j)),
                  pl.BlockSpec((1, tn), lambda j: (0, j))],
        out_specs=pl.BlockSpec((nb, tn), lambda j: (0, j)),
        out_shape=jax.ShapeDtypeStruct((nb, n), F32),
        compiler_params=_cparams(("arbitrary",)),
        name="ada_mod",
    )(c, w_ada, b_ada.reshape(1, n))


def _norm_mod_kernel(x_ref, g_ref, mod_ref, o_ref):
    x = x_ref[0]
    y = x * lax.rsqrt(jnp.mean(x * x, axis=-1, keepdims=True) + RMS_EPS) * g_ref[...]
    shift = mod_ref[0, 0:1, :]
    scale = mod_ref[0, 1:2, :]
    o_ref[0] = (y * (1.0 + scale) + shift).astype(o_ref.dtype)


def norm_mod(x, g, mod3):
    b, l, d = x.shape
    tm = _tile(l, 256)
    return pl.pallas_call(
        _norm_mod_kernel,
        grid=(b, l // tm),
        in_specs=[pl.BlockSpec((1, tm, d), lambda i, t: (i, t, 0)),
                  pl.BlockSpec((1, d), lambda i, t: (0, 0)),
                  pl.BlockSpec((1, 3, d), lambda i, t: (i, 0, 0))],
        out_specs=pl.BlockSpec((1, tm, d), lambda i, t: (i, t, 0)),
        out_shape=jax.ShapeDtypeStruct((b, l, d), BF16),
        compiler_params=_cparams(("parallel", "parallel")),
        name="norm_mod",
    )(x, g.reshape(1, d), mod3)


def _final_norm_kernel(x_ref, g_ref, o_ref):
    x = x_ref[0]
    o_ref[0] = x * lax.rsqrt(jnp.mean(x * x, axis=-1, keepdims=True) + RMS_EPS) * g_ref[...]


def final_norm(x, g):
    b, l, d = x.shape
    tm = _tile(l, 256)
    return pl.pallas_call(
        _final_norm_kernel,
        grid=(b, l // tm),
        in_specs=[pl.BlockSpec((1, tm, d), lambda i, t: (i, t, 0)),
                  pl.BlockSpec((1, d), lambda i, t: (0, 0))],
        out_specs=pl.BlockSpec((1, tm, d), lambda i, t: (i, t, 0)),
        out_shape=jax.ShapeDtypeStruct((b, l, d), F32),
        compiler_params=_cparams(("parallel", "parallel")),
        name="final_norm",
    )(x, g.reshape(1, d))


def _mm_kernel(a_ref, b_ref, o_ref):
    o_ref[...] = jnp.dot(a_ref[...], b_ref[...], preferred_element_type=F32).astype(o_ref.dtype)


def matmul(a, w, n_out, src_block, tn, out_dtype, name):
    m, k = a.shape
    tm = _tile(m, 1024)
    return pl.pallas_call(
        _mm_kernel,
        grid=(m // tm, n_out // tn),
        in_specs=[pl.BlockSpec((tm, k), lambda i, j: (i, 0)),
                  pl.BlockSpec((k, tn), lambda i, j: (0, src_block(j)))],
        out_specs=pl.BlockSpec((tm, tn), lambda i, j: (i, j)),
        out_shape=jax.ShapeDtypeStruct((m, n_out), out_dtype),
        compiler_params=_cparams(("parallel", "arbitrary")),
        name=name,
    )(a, w)


def in_proj(h2d, w_in, d):
    n_main = PM_G + N_BRANCH * d
    tn = _tile(n_main, 512)
    assert BRANCH_W % tn == 0
    per = BRANCH_W // tn
    ng = len(A_GROUPS)

    def main_src(j):
        return jnp.where(j < 3 * per, (j // per) * ng * per + j % per,
                         jnp.where(j < PM_CZ // tn, j + (OFF_AZ - PM_AZ) // tn, j + (OFF_CZ - PM_CZ) // tn))

    def side_src(j):
        c = j // per
        return jnp.where(j < SIDE_C // tn, ((c // 2) * ng + 1 + c % 2) * per + j % per, j + (OFF_C - SIDE_C) // tn)

    main = matmul(h2d, w_in, n_main, main_src, tn, BF16, "in_proj")
    side = matmul(h2d, w_in, SIDE_COLS, side_src, tn, F32, "in_proj_side")
    return main, side


A_UNROLL = 8


def _attn_a_kernel(slopes_ref, *refs, seq):
    qkv0 = refs[0:3]
    qkvd = refs[3:9]
    z_ref, o_ref = refs[9], refs[10]
    bias_sc, o_sc, lse_sc = refs[11:]
    h = pl.program_id(1)
    scale = HEAD_DIM ** -0.5

    for g, (_, dil) in enumerate(A_GROUPS):
        n = seq // dil
        qb = min(A_QB, n)
        slab = min(qb + 2 * A_RADIUS, n)
        nqb = n // qb
        if dil == 1:
            q_ref, k_ref, v_ref = (r.at[0] for r in qkv0)
        else:
            q_ref, k_ref, v_ref = (qkvd[s * 2 + g - 1].at[0] for s in range(3))
        slope = slopes_ref[g * A_HEADS + h] * float(dil)

        col_minus_row = (lax.broadcasted_iota(jnp.int32, (qb, slab), 1)
                         - lax.broadcasted_iota(jnp.int32, (qb, slab), 0))
        for cls, off in enumerate((0, -A_RADIUS, qb - slab)):
            dist = jnp.abs(col_minus_row + off)
            bias_sc[cls, 0:qb, 0:slab] = jnp.where(dist <= A_RADIUS, -slope * dist.astype(F32), NEG_INF)

        def rows_of(ref, start, size, r, dil=dil):
            if dil == 1:
                return ref[pl.ds(pl.multiple_of(start, 64), size), :]
            return ref[pl.ds(start * dil + r, size, stride=dil), :].astype(BF16)

        def body(it, carry, g=g, dil=dil, n=n, qb=qb, slab=slab, nqb=nqb, rows_of=rows_of,
                 q_ref=q_ref, k_ref=k_ref, v_ref=v_ref):
            blks = [it * A_UNROLL + u for u in range(A_UNROLL)]
            res = [blk // nqb for blk in blks]
            q0s = [(blk % nqb) * qb for blk in blks]
            kss = [jnp.clip(q0 - A_RADIUS, 0, n - slab) for q0 in q0s]
            scores = [lax.dot_general(rows_of(q_ref, q0, qb, r), rows_of(k_ref, ks, slab, r),
                                      (((1,), (1,)), ((), ())), preferred_element_type=F32)
                      for q0, ks, r in zip(q0s, kss, res)]
            probs, stats = [], []
            for s, q0 in zip(scores, q0s):
                cls = jnp.where(q0 == 0, 0, jnp.where(q0 + qb == n, 2, 1))
                s = s * scale + bias_sc[cls, 0:qb, 0:slab]
                m = jnp.max(s, axis=-1, keepdims=True)
                p = jnp.exp(s - m)
                l = jnp.sum(p, axis=-1, keepdims=True)
                probs.append(p.astype(BF16))
                stats.append((m, l))
            outs = [jnp.dot(p, rows_of(v_ref, ks, slab, r), preferred_element_type=F32)
                    for p, ks, r in zip(probs, kss, res)]
            for o, (m, l), q0, r in zip(outs, stats, q0s, res):
                if dil == 1:
                    rows = pl.ds(pl.multiple_of(q0, 64), qb)
                else:
                    rows = pl.ds(q0 * dil + r, qb, stride=dil)
                o_sc[g, rows, :] = o * (1.0 / l)
                lse_sc[g, rows, :] = jnp.broadcast_to(m + jnp.log(l), (qb, LANES))
            return carry

        lax.fori_loop(0, (dil * nqb) // A_UNROLL, body, 0)

    ch = min(256, seq)

    def merge(c, carry):
        rows = pl.ds(pl.multiple_of(c * ch, ch), ch)
        l0, l1, l2 = lse_sc[0, rows, :], lse_sc[1, rows, :], lse_sc[2, rows, :]
        mx = jnp.maximum(jnp.maximum(l0, l1), l2)
        e0, e1, e2 = jnp.exp(l0 - mx), jnp.exp(l1 - mx), jnp.exp(l2 - mx)
        y = (e0 * o_sc[0, rows, :] + e1 * o_sc[1, rows, :] + e2 * o_sc[2, rows, :]) * (1.0 / (e0 + e1 + e2))
        o_ref[0, rows, :] = (y * _silu(z_ref[0, rows, :].astype(F32))).astype(o_ref.dtype)
        return carry

    lax.fori_loop(0, seq // ch, merge, 0)


def attn_a(proj, proj_dil, slopes):
    b, l, _ = proj.shape
    ng = len(A_GROUPS)
    assert (l // A_QB) % A_UNROLL == 0

    def spec(blk0):
        return pl.BlockSpec((1, l, HEAD_DIM), lambda i, h, blk0=blk0: (i, 0, blk0 + h))

    in_specs = [pl.BlockSpec(memory_space=pltpu.SMEM)]
    in_specs += [spec(s * A_HEADS) for s in range(3)]
    in_specs += [spec(c * A_HEADS) for c in range(3 * (ng - 1))]
    in_specs += [spec(PM_AZ // HEAD_DIM)]
    return pl.pallas_call(
        functools.partial(_attn_a_kernel, seq=l),
        grid=(b, A_HEADS),
        in_specs=in_specs,
        out_specs=pl.BlockSpec((1, l, HEAD_DIM), lambda i, h: (i, 0, h)),
        out_shape=jax.ShapeDtypeStruct((b, l, BRANCH_W), BF16),
        scratch_shapes=[pltpu.VMEM((3, A_QB, A_QB + 2 * A_RADIUS), F32),
                        pltpu.VMEM((ng, l, LANES), F32), pltpu.VMEM((ng, l, LANES), F32)],
        compiler_params=_cparams(("parallel", "parallel")),
        name="attn_dilated",
    )(slopes, proj, proj, proj, *([proj_dil] * (3 * (ng - 1))), proj)


def _na_bias_tables(rpb, rows):
    kr = min(NA_KR, rows)
    qc = np.arange(GRID_W)
    cs = np.clip(qc - NA_KC // 2, 0, GRID_W - NA_KC)
    col_ok = (qc[None, :] >= cs[:, None]) & (qc[None, :] < cs[:, None] + NA_KC)
    dc = np.clip(qc[None, :] - qc[:, None] + NA_KC - 1, 0, 2 * NA_KC - 2)
    dc_onehot = (dc[..., None] == np.arange(2 * NA_KC - 1)).astype(np.float32)
    tabs = []
    for a in (0, 1, rows // NA_QROWS - 1):
        ks = int(np.clip(NA_QROWS * a - NA_QROWS, 0, rows - NA_KROWS))
        rq = NA_QROWS * a + np.arange(NA_QROWS)
        rs = np.clip(rq - kr // 2, 0, rows - kr)
        rk = ks + np.arange(NA_KROWS)
        row_ok = (rk[None, :] >= rs[:, None]) & (rk[None, :] < rs[:, None] + kr)
        dr = np.clip(rk[None, :] - rq[:, None] + NA_KR - 1, 0, 2 * NA_KR - 2)
        dr_onehot = (dr[..., None] == np.arange(2 * NA_KR - 1)).astype(np.float32)
        t = jnp.einsum("qjr,ckd,hrd->hqcjk", dr_onehot, dc_onehot, rpb.astype(F32),
                       precision=lax.Precision.HIGHEST)
        ok = row_ok[:, None, :, None] & col_ok[None, :, None, :]
        tabs.append(jnp.where(ok[None], t, NEG_INF).reshape(-1, NA_QT, NA_KT))
    return jnp.stack(tabs, axis=1)


NA_UNROLL = 2


def _attn_b_kernel(q_ref, k_ref, v_ref, z_ref, bias_ref, o_ref, *, seq):
    rows = seq // GRID_W
    nblk = rows // NA_QROWS
    scale = HEAD_DIM ** -0.5

    def body(it, carry):
        blks = [it * NA_UNROLL + u for u in range(NA_UNROLL)]
        ks_rows = [jnp.clip(NA_QROWS * a - NA_QROWS, 0, rows - NA_KROWS) for a in blks]
        qrows = [pl.ds(pl.multiple_of(a * NA_QT, NA_QT), NA_QT) for a in blks]
        krows = [pl.ds(pl.multiple_of(ks * GRID_W, GRID_W), NA_KT) for ks in ks_rows]
        scores = [lax.dot_general(q_ref[0, qr, :], k_ref[0, kr, :], (((1,), (1,)), ((), ())),
                                  preferred_element_type=F32) for qr, kr in zip(qrows, krows)]
        probs, sums = [], []
        for s, a, ks in zip(scores, blks, ks_rows):
            s = s * scale + bias_ref[0, (NA_QROWS * a - ks) // NA_QROWS]
            p = jnp.exp(s - jnp.max(s, axis=-1, keepdims=True))
            sums.append(jnp.sum(p, axis=-1, keepdims=True))
            probs.append(p.astype(BF16))
        outs = [jnp.dot(p, v_ref[0, kr, :], preferred_element_type=F32) for p, kr in zip(probs, krows)]
        for o, l, qr in zip(outs, sums, qrows):
            o_ref[0, qr, :] = (o * (1.0 / l) * _silu(z_ref[0, qr, :].astype(F32))).astype(o_ref.dtype)
        return carry

    lax.fori_loop(0, nblk // NA_UNROLL, body, 0)


def attn_b(proj, bias):
    b, l, _ = proj.shape
    assert l % (NA_QT * NA_UNROLL) == 0 and l // GRID_W >= NA_KROWS

    def col_spec(off):
        return pl.BlockSpec((1, l, HEAD_DIM), lambda h, i, off=off: (i, 0, off // HEAD_DIM + h))

    return pl.pallas_call(
        functools.partial(_attn_b_kernel, seq=l),
        grid=(B_HEADS, b),
        in_specs=[col_spec(PM_B), col_spec(PM_B + B_HEADS * HEAD_DIM),
                  col_spec(PM_B + 2 * B_HEADS * HEAD_DIM), col_spec(PM_BZ),
                  pl.BlockSpec((1, 3, NA_QT, NA_KT), lambda h, i: (h, 0, 0, 0))],
        out_specs=pl.BlockSpec((1, l, HEAD_DIM), lambda h, i: (i, 0, h)),
        out_shape=jax.ShapeDtypeStruct((b, l, BRANCH_W), BF16),
        compiler_params=_cparams(("parallel", "parallel")),
        name="attn_neighbourhood",
    )(proj, proj, proj, proj, bias)


HY_TAB = 6
HY_ROW0 = 8


def _shift_down(x):
    t = lax.broadcasted_iota(jnp.int32, x.shape, 0)
    return jnp.where(t == 0, 0.0, pltpu.roll(x, 1, axis=0))


def _shift_up(x):
    n = x.shape[0]
    t = lax.broadcasted_iota(jnp.int32, x.shape, 0)
    return jnp.where(t == n - 1, 0.0, pltpu.roll(x, n - 1, axis=0))


def _conv3_kernel(u_ref, w_ref, b_ref, o_ref, *, half):
    e = u_ref[0, pl.ds(0, half, stride=2), :]
    o = u_ref[0, pl.ds(1, half, stride=2), :]
    w0, w1, w2, bias = w_ref[0:1, :], w_ref[1:2, :], w_ref[2:3, :], b_ref[...]
    o_ref[0, 0] = (w0 * _shift_down(o) + w1 * e + w2 * o + bias).astype(o_ref.dtype)
    o_ref[0, 1] = (w0 * e + w1 * o + w2 * _shift_up(e) + bias).astype(o_ref.dtype)


def conv3(side, conv_w, conv_b):
    b, l, _ = side.shape
    half = l // 2
    return pl.pallas_call(
        functools.partial(_conv3_kernel, half=half),
        grid=(b, C_PROJ // LANES),
        in_specs=[pl.BlockSpec((1, l, LANES), lambda i, j: (i, 0, SIDE_C // LANES + j)),
                  pl.BlockSpec((3, LANES), lambda i, j: (0, j)),
                  pl.BlockSpec((1, LANES), lambda i, j: (0, j))],
        out_specs=pl.BlockSpec((1, 2, half, LANES), lambda i, j: (i, 0, 0, j)),
        out_shape=jax.ShapeDtypeStruct((b, 2, half, C_PROJ), BF16),
        compiler_params=_cparams(("parallel", "parallel")),
        name="hyena_conv3",
    )(side, conv_w, conv_b.reshape(1, C_PROJ))


def _filter_kernel(feats_ref, w1_ref, b1_ref, w2_ref, b2_ref, freq_ref, w3_ref, dec_ref, o_ref, hid_ref,
                   *, tc):
    j = pl.program_id(0)
    hp = lax.Precision.HIGHEST

    @pl.when(j == 0)
    def _():
        h1 = jnp.sin(freq_ref[0:1, :] * (jnp.dot(feats_ref[...], w1_ref[...], precision=hp,
                                                 preferred_element_type=F32) + b1_ref[...]))
        hid_ref[...] = jnp.sin(freq_ref[1:2, :] * (jnp.dot(h1, w2_ref[...], precision=hp,
                                                           preferred_element_type=F32) + b2_ref[...]))

    raw = jnp.dot(hid_ref[...], w3_ref[...], precision=hp, preferred_element_type=F32)
    t_norm = feats_ref[:, 0:1]
    hfil = raw * jnp.exp(-t_norm * jnp.exp(dec_ref[...]))
    hfil = hfil * (1.0 / (jnp.sum(jnp.abs(hfil), axis=0, keepdims=True) + 1e-6))
    dropped_row = ((j * tc) // HY_CH) % 2 - 1
    t = lax.broadcasted_iota(jnp.int32, hfil.shape, 0)
    o_ref[...] = jnp.where(t == dropped_row, 0.0, hfil).astype(o_ref.dtype)


def hyena_filter_taps(seq, w1, b1, w2, b2, w3, freq, log_decay):
    t = jnp.concatenate([jnp.arange(0, seq, 2), jnp.arange(1, seq, 2)]).astype(F32)
    t_norm = t / (seq - 1)
    ang = (2.0 * math.pi / seq) * t
    bands = jnp.linspace(1e-4, HY_BANDS - 1, HY_BANDS, dtype=F32)
    feats = jnp.concatenate([t_norm[:, None], jnp.cos(ang[:, None] * bands), -jnp.sin(ang[:, None] * bands)], axis=-1)
    emb = 64
    feats = jnp.pad(feats, ((0, 0), (0, emb - HY_EMB)))
    w1p = jnp.pad(w1, ((0, emb - HY_EMB), (0, 0)))
    ncol = HY_ORDER * 2 * HY_CH
    tc = 256
    full = lambda shape: pl.BlockSpec(shape, lambda j: (0, 0))
    return pl.pallas_call(
        functools.partial(_filter_kernel, tc=tc),
        grid=(ncol // tc,),
        in_specs=[full((seq, emb)), full((emb, HY_FFN)), full((1, HY_FFN)), full((HY_FFN, HY_FFN)),
                  full((1, HY_FFN)), full((2, HY_FFN)),
                  pl.BlockSpec((HY_FFN, tc), lambda j: (0, j)),
                  pl.BlockSpec((1, tc), lambda j: (0, j))],
        out_specs=pl.BlockSpec((seq, tc), lambda j: (0, j)),
        out_shape=jax.ShapeDtypeStruct((seq, ncol), BF16),
        scratch_shapes=[pltpu.VMEM((seq, HY_FFN), F32)],
        compiler_params=_cparams(("arbitrary",)),
        name="hyena_filter_taps",
    )(feats, w1p, b1.reshape(1, HY_FFN), w2, b2.reshape(1, HY_FFN), freq, w3, log_decay.reshape(1, ncol))


def _dft_consts(seq):
    half = seq // 2
    k = jnp.arange(half, dtype=jnp.int32)
    ang = ((k[:, None] * k[None, :]) % seq).astype(F32) * (2.0 * math.pi / seq)
    cmat = jnp.cos(ang)
    smat = jnp.sin(ang)
    sign = (1 - 2 * (k % 2)).astype(F32)
    smat = jnp.where((k == 0)[:, None], sign[None, :], smat)
    tw = k.astype(F32) * (2.0 * math.pi / seq)
    c2 = jnp.broadcast_to(jnp.cos(tw)[:, None], (half, LANES))
    s2 = jnp.broadcast_to(jnp.sin(tw)[:, None], (half, LANES))
    return dict(cmat=cmat.astype(BF16), smat=smat.astype(BF16), smat_t=smat.T.astype(BF16), c2=c2, s2=s2)


def _half_transforms(c_ref, s_ref, x_ref):
    cm, sm = c_ref[...], s_ref[...]
    e, o = x_ref[0], x_ref[1]
    dot = lambda a, b: jnp.dot(a, b, preferred_element_type=F32)
    return dot(cm, e), dot(sm, e), dot(cm, o), dot(sm, o)


def _filter_spec_kernel(c_ref, s_ref, c2_ref, s2_ref, xf_ref, xb_ref, tab_ref, row0_ref, *, seq, tf, tc):
    i = pl.program_id(2)
    pef, qef, pof, qof = _half_transforms(c_ref, s_ref, xf_ref)
    peb, qeb, pob, qob = _half_transforms(c_ref, s_ref, xb_ref)
    c2 = jnp.tile(c2_ref[...], (1, tc // LANES))
    s2 = jnp.tile(s2_ref[...], (1, tc // LANES))
    sc = 2.0 * (2.0 / (2 * seq))
    tab_ref[0, 0] = sc * (pef + peb)
    tab_ref[0, 1] = sc * (qeb - qef)
    tab_ref[0, 2] = sc * (c2 * pof - s2 * qof + pob)
    tab_ref[0, 3] = sc * (qob - c2 * qof - s2 * pof)
    tab_ref[0, 4] = sc * (pof + c2 * pob - s2 * qob)
    tab_ref[0, 5] = sc * (c2 * qob + s2 * pob - qof)

    @pl.when(i == 0)
    def _():
        inv_n = 1.0 / (2 * seq)
        row0_ref[0] = jnp.zeros(row0_ref.shape[1:], F32)
        row0_ref[0, 0:1, :] = 2.0 * inv_n * (pef[0:1] + peb[0:1])
        row0_ref[0, 1:2, :] = 2.0 * inv_n * (pof[0:1] + pob[0:1])
        row0_ref[0, 2:3, :] = 2.0 * inv_n * (qef[0:1] + qeb[0:1])
        row0_ref[0, 3:4, :] = 2.0 * inv_n * (qob[0:1] - qof[0:1])


def hyena_filter_spectrum(taps, dc):
    seq = taps.shape[0]
    half = seq // 2
    taps = taps.reshape(2, half, taps.shape[1])
    tf, tc = _tile(half, 256), 512
    ncb = HY_CH // tc
    mspec = pl.BlockSpec((tf, half), lambda o, j, i: (i, 0))
    tspec = pl.BlockSpec((tf, LANES), lambda o, j, i: (i, 0))
    return pl.pallas_call(
        functools.partial(_filter_spec_kernel, seq=seq, tf=tf, tc=tc),
        grid=(HY_ORDER, ncb, half // tf),
        in_specs=[mspec, mspec, tspec, tspec,
                  pl.BlockSpec((2, half, tc), lambda o, j, i: (0, 0, (o * 2) * ncb + j)),
                  pl.BlockSpec((2, half, tc), lambda o, j, i: (0, 0, (o * 2 + 1) * ncb + j))],
        out_specs=[pl.BlockSpec((1, HY_TAB, tf, tc), lambda o, j, i: (o, 0, i, j)),
                   pl.BlockSpec((1, HY_ROW0, tc), lambda o, j, i: (o, 0, j))],
        out_shape=[jax.ShapeDtypeStruct((HY_ORDER, HY_TAB, half, HY_CH), F32),
                   jax.ShapeDtypeStruct((HY_ORDER, HY_ROW0, HY_CH), F32)],
        compiler_params=_cparams(("parallel", "parallel", "arbitrary")),
        name="hyena_filter_spectrum",
    )(dc["cmat"], dc["smat"], dc["c2"], dc["s2"], taps, taps)


def _conv_fwd_kernel(c_ref, s_ref, z_ref, tab_ref, row0_ref, uv_ref, *, tf):
    i = pl.program_id(1)
    pe, qe, po, qo = _half_transforms(c_ref, s_ref, z_ref.at[0])
    spr, spi, wmr, wmi, vmr, vmi = (tab_ref[0, n] for n in range(HY_TAB))
    u0 = pe * spr + qe * spi + po * wmr + qo * wmi
    v0 = qe * spr - pe * spi + qo * wmr - po * wmi
    u1 = pe * vmr + qe * vmi + po * spr + qo * spi
    v1 = qe * vmr - pe * vmi + qo * spr - po * spi
    ra, rb, rc, rd = (row0_ref[0, n:n + 1, :] for n in range(4))
    first = (lax.broadcasted_iota(jnp.int32, u0.shape, 0) + i * tf) == 0
    pe0, qe0, po0, qo0 = pe[0:1], qe[0:1], po[0:1], qo[0:1]
    uv_ref[0, 0] = jnp.where(first, pe0 * ra + po0 * rb, u0).astype(uv_ref.dtype)
    uv_ref[0, 1] = jnp.where(first, qe0 * rc + qo0 * rd, v0).astype(uv_ref.dtype)
    uv_ref[0, 2] = jnp.where(first, pe0 * rb + po0 * ra, u1).astype(uv_ref.dtype)
    uv_ref[0, 3] = jnp.where(first, qo0 * rc - qe0 * rd, v1).astype(uv_ref.dtype)


def conv_fwd(z, zcol, dc, tabs, row0, order):
    b, _, half, _ = z.shape
    tf, tc = _tile(half, 256), 512
    ncb = HY_CH // tc
    mspec = pl.BlockSpec((tf, half), lambda j, i, n: (i, 0))
    return pl.pallas_call(
        functools.partial(_conv_fwd_kernel, tf=tf),
        grid=(ncb, half // tf, b),
        in_specs=[mspec, mspec,
                  pl.BlockSpec((1, 2, half, tc), lambda j, i, n: (n, 0, 0, zcol // tc + j)),
                  pl.BlockSpec((1, HY_TAB, tf, tc), lambda j, i, n: (order, 0, i, j)),
                  pl.BlockSpec((1, HY_ROW0, tc), lambda j, i, n: (order, 0, j))],
        out_specs=pl.BlockSpec((1, 4, tf, tc), lambda j, i, n: (n, 0, i, j)),
        out_shape=jax.ShapeDtypeStruct((b, 4, half, HY_CH), BF16),
        compiler_params=_cparams(("parallel", "parallel", "arbitrary")),
        name="hyena_conv_fwd",
    )(dc["cmat"], dc["smat"], z, tabs, row0)


def _conv_inv_kernel(c_ref, st_ref, uv_ref, x_ref, zp_ref, skip_ref, *rest, gated, tt, tc):
    cm, sm = c_ref[...], st_ref[...]
    dot = lambda a, b: jnp.dot(a, b, preferred_element_type=F32)
    skip = skip_ref[0]
    halves = []
    for par in range(2):
        y = dot(cm, uv_ref[0, 2 * par]) + dot(sm, uv_ref[0, 2 * par + 1])
        halves.append(x_ref[0, par].astype(F32) * (y + zp_ref[0, par].astype(F32) * skip))
    if not gated:
        o_ref = rest[0]
        o_ref[0, 0] = halves[0].astype(o_ref.dtype)
        o_ref[0, 1] = halves[1].astype(o_ref.dtype)
        return
    gate_ref, o_ref, il_ref = rest
    for s in range(tc // LANES):
        lanes = slice(s * LANES, (s + 1) * LANES)
        il_ref[s, pl.ds(0, tt, stride=2), :] = halves[0][:, lanes]
        il_ref[s, pl.ds(1, tt, stride=2), :] = halves[1][:, lanes]
    for s in range(tc // LANES):
        lanes = slice(s * LANES, (s + 1) * LANES)
        o_ref[0, :, lanes] = (il_ref[s] * _silu(gate_ref[0, :, lanes].astype(F32))).astype(o_ref.dtype)


def conv_inv(uv, dc, xmul, xcol, zprev, zcol, skip, gate=None, gcol=0):
    b, _, half, _ = uv.shape
    tt, tc = _tile(half, 256), 512
    ncb = HY_CH // tc
    mspec = pl.BlockSpec((tt, half), lambda n, j, i: (i, 0))

    def tile_spec(col):
        return pl.BlockSpec((1, 2, tt, tc), lambda n, j, i, col=col: (n, 0, i, col // tc + j))

    in_specs = [mspec, mspec, pl.BlockSpec((1, 4, half, tc), lambda n, j, i: (n, 0, 0, j)),
                tile_spec(xcol), tile_spec(zcol), pl.BlockSpec((1, 1, tc), lambda n, j, i: (0, 0, j))]
    args = [dc["cmat"], dc["smat_t"], uv, xmul, zprev, skip.reshape(1, 1, HY_CH)]
    scratch = []
    if gate is None:
        out_spec = pl.BlockSpec((1, 2, tt, tc), lambda n, j, i: (n, 0, i, j))
        out_shape = jax.ShapeDtypeStruct((b, 2, half, HY_CH), BF16)
    else:
        in_specs.append(pl.BlockSpec((1, 2 * tt, tc), lambda n, j, i: (n, i, gcol // tc + j)))
        args.append(gate)
        out_spec = pl.BlockSpec((1, 2 * tt, tc), lambda n, j, i: (n, i, j))
        out_shape = jax.ShapeDtypeStruct((b, 2 * half, HY_CH), BF16)
        scratch = [pltpu.VMEM((tc // LANES, 2 * tt, LANES), F32)]
    return pl.pallas_call(
        functools.partial(_conv_inv_kernel, gated=gate is not None, tt=tt, tc=tc),
        grid=(b, ncb, half // tt),
        in_specs=in_specs,
        out_specs=out_spec,
        out_shape=out_shape,
        scratch_shapes=scratch,
        compiler_params=_cparams(("parallel", "parallel", "arbitrary")),
        name="hyena_conv_inv",
    )(*args)


def mixer_hyena(proj, side, dc, tabs, row0, conv_w, conv_b, skip):
    u = conv3(side, conv_w, conv_b)
    uv = conv_fwd(u, 0, dc, tabs, row0, 0)
    z1 = conv_inv(uv, dc, u, HY_CH, u, 0, skip[0])
    uv = conv_fwd(z1, 0, dc, tabs, row0, 1)
    return conv_inv(uv, dc, u, 2 * HY_CH, z1, 0, skip[1], gate=proj, gcol=PM_CZ)


def _merge_kernel(ya_ref, yb_ref, yc_ref, w_ref, g0_ref, g1_ref, g2_ref, o_ref):
    acc = None
    for y_ref, g_ref, n in ((ya_ref, g0_ref, 0), (yb_ref, g1_ref, 1), (yc_ref, g2_ref, 2)):
        t = _sigmoid(g_ref[...].astype(F32)) * jnp.dot(y_ref[...], w_ref[n], preferred_element_type=F32)
        acc = t if acc is None else acc + t
    o_ref[...] = acc.astype(o_ref.dtype)


def merge_branches(ya, yb, yc, w_branch, proj2d, d):
    m = ya.shape[0]
    tm, tn = _tile(m, 1024), _tile(d, 512)
    yspec = pl.BlockSpec((tm, BRANCH_W), lambda i, j: (i, 0))

    def gspec(n):
        return pl.BlockSpec((tm, tn), lambda i, j, n=n: (i, (PM_G + n * d) // tn + j))

    return pl.pallas_call(
        _merge_kernel,
        grid=(m // tm, d // tn),
        in_specs=[yspec, yspec, yspec,
                  pl.BlockSpec((N_BRANCH, BRANCH_W, tn), lambda i, j: (0, 0, j)),
                  gspec(0), gspec(1), gspec(2)],
        out_specs=pl.BlockSpec((tm, tn), lambda i, j: (i, j)),
        out_shape=jax.ShapeDtypeStruct((m, d), BF16),
        compiler_params=_cparams(("parallel", "arbitrary")),
        name="branch_merge",
    )(ya, yb, yc, w_branch, proj2d, proj2d, proj2d)


def _out_kernel(a_ref, w_ref, x_ref, mod_ref, o_ref):
    acc = jnp.dot(a_ref[0], w_ref[...], preferred_element_type=F32)
    o_ref[0] = x_ref[0] + mod_ref[0, 2:3, :] * acc


def out_proj(mixed, w_out, x, mod3):
    b, l, d = x.shape
    tm, tn = _tile(l, 1024), _tile(d, 512)
    return pl.pallas_call(
        _out_kernel,
        grid=(b, l // tm, d // tn),
        in_specs=[pl.BlockSpec((1, tm, d), lambda n, i, j: (n, i, 0)),
                  pl.BlockSpec((d, tn), lambda n, i, j: (0, j)),
                  pl.BlockSpec((1, tm, tn), lambda n, i, j: (n, i, j)),
                  pl.BlockSpec((1, 3, tn), lambda n, i, j: (n, 0, j))],
        out_specs=pl.BlockSpec((1, tm, tn), lambda n, i, j: (n, i, j)),
        out_shape=jax.ShapeDtypeStruct((b, l, d), F32),
        compiler_params=_cparams(("parallel", "parallel", "arbitrary")),
        name="out_proj",
    )(mixed, w_out, x, mod3)


def _trunk(x, mods, layers, dc, slopes, final_g):
    b, l, d = x.shape
    for lw, mod3 in zip(layers, mods):
        h = norm_mod(x, lw["norm_g"], mod3)
        proj2d, side2d = in_proj(h.reshape(b * l, d), lw["w_in"], d)
        proj = proj2d.reshape(b, l, -1)
        side = side2d.reshape(b, l, -1)
        ya = attn_a(proj, side, slopes)
        yb = attn_b(proj, _na_bias_tables(lw["na_rpb"], l // GRID_W))
        taps = hyena_filter_taps(l, lw["hy_w1"], lw["hy_b1"], lw["hy_w2"], lw["hy_b2"], lw["hy_w3"],
                                 lw["hy_freq"], lw["hy_log_decay"])
        tabs, row0 = hyena_filter_spectrum(taps, dc)
        yc = mixer_hyena(proj, side, dc, tabs, row0, lw["conv_w"], lw["conv_b"], lw["hy_skip"])
        mixed = merge_branches(ya.reshape(b * l, -1), yb.reshape(b * l, -1), yc.reshape(b * l, -1),
                               lw["w_branch"], proj2d, d)
        x = out_proj(mixed.reshape(b, l, d), lw["w_out"], x, mod3)
    return final_norm(x, final_g)


def kernel(x_prompt, x_sample, c_prompt, c_sample, norm_g, w_ada, b_ada, w_in, w_branch, w_out, na_rpb, conv_w, conv_b, hy_w1, hy_b1, hy_w2, hy_b2, hy_w3, hy_freq, hy_log_decay, hy_skip, final_g):
    depth, d = norm_g.shape
    nbp = c_prompt.shape[0]
    c_all = jnp.concatenate([c_prompt, c_sample], axis=0)
    nslope = len(A_GROUPS) * A_HEADS
    slopes = jnp.asarray(2.0 ** (-8.0 * (np.arange(nslope) + 1) / nslope), F32)
    layers, mods_p, mods_s = [], [], []
    for i in range(depth):
        mod3 = ada_mod(c_all, w_ada[i], b_ada[i]).reshape(c_all.shape[0], 3, d)
        mods_p.append(mod3[:nbp])
        mods_s.append(mod3[nbp:])
        layers.append(dict(
            norm_g=norm_g[i], w_in=w_in[i].astype(BF16), w_branch=w_branch[i].astype(BF16),
            w_out=w_out[i].astype(BF16), na_rpb=na_rpb[i], conv_w=conv_w[i], conv_b=conv_b[i],
            hy_w1=hy_w1[i], hy_b1=hy_b1[i], hy_w2=hy_w2[i], hy_b2=hy_b2[i], hy_w3=hy_w3[i],
            hy_freq=hy_freq[i], hy_log_decay=hy_log_decay[i], hy_skip=hy_skip[i]))
    outs = []
    for x, mods in ((x_prompt, mods_p), (x_sample, mods_s)):
        outs.append(_trunk(x, mods, layers, _dft_consts(x.shape[1]), slopes, final_g))
    return tuple(outs)
```

```python
import functools
import math

import jax
import jax.numpy as jnp
import numpy as np
from jax import lax
from jax.experimental import pallas as pl
from jax.experimental.pallas import tpu as pltpu

F32 = jnp.float32
BF16 = jnp.bfloat16

LANES = 128
V7X_VMEM_LIMIT = 56 * 1024 * 1024

HEAD_DIM = 128
GRID_W = 64
BRANCH_W = 1024
N_BRANCH = 3
A_GROUPS = ((128, 1), (512, 4), (2048, 16))
A_HEADS = 8
A_RADIUS = 64
B_HEADS = 8
NA_KR = 8
NA_KC = 16
HY_ORDER = 2
HY_CH = BRANCH_W
HY_BANDS = 16
HY_EMB = 1 + 2 * HY_BANDS
HY_FFN = 64
RMS_EPS = 1e-6
NEG_INF = -1e30

A_QKV = 3 * len(A_GROUPS) * A_HEADS * HEAD_DIM
B_QKV = 3 * B_HEADS * HEAD_DIM
C_PROJ = (HY_ORDER + 1) * HY_CH
OFF_AZ = A_QKV
OFF_B = OFF_AZ + BRANCH_W
OFF_BZ = OFF_B + B_QKV
OFF_C = OFF_BZ + BRANCH_W
OFF_CZ = OFF_C + C_PROJ
OFF_G = OFF_CZ + BRANCH_W

PM_AZ = 3 * BRANCH_W
PM_B = PM_AZ + BRANCH_W
PM_BZ = PM_B + B_QKV
PM_CZ = PM_BZ + BRANCH_W
PM_G = PM_CZ + BRANCH_W
SIDE_C = 3 * (len(A_GROUPS) - 1) * BRANCH_W
SIDE_COLS = SIDE_C + C_PROJ

NA_QROWS = 4
NA_KROWS = 12
NA_QT = NA_QROWS * GRID_W
NA_KT = NA_KROWS * GRID_W

A_QB = 128


def _cparams(sem):
    return pltpu.CompilerParams(dimension_semantics=sem, vmem_limit_bytes=V7X_VMEM_LIMIT)


def _tile(n, pref):
    t = min(n, pref)
    while n % t:
        t //= 2
    return t


def _silu(z):
    return z * (1.0 / (1.0 + jnp.exp(-z)))


def _sigmoid(z):
    return 1.0 / (1.0 + jnp.exp(-z))


def _ada_kernel(c_ref, w_ref, b_ref, o_ref):
    a = _silu(c_ref[...]).astype(BF16)
    o_ref[...] = jnp.dot(a, w_ref[0].astype(BF16), preferred_element_type=F32) + b_ref[0]


def ada_mod(c, w_ada, b_ada, layer):
    nb, d = c.shape
    depth, _, n = w_ada.shape
    tn = _tile(n, 512)
    return pl.pallas_call(
        _ada_kernel,
        grid=(n // tn,),
        in_specs=[pl.BlockSpec((nb, d), lambda j: (0, 0)),
                  pl.BlockSpec((1, d, tn), lambda j: (layer, 0, j)),
                  pl.BlockSpec((1, 1, tn), lambda j: (layer, 0, j))],
        out_specs=pl.BlockSpec((nb, tn), lambda j: (0, j)),
        out_shape=jax.ShapeDtypeStruct((nb, n), F32),
        compiler_params=_cparams(("arbitrary",)),
        name="ada_mod",
    )(c, w_ada, b_ada.reshape(depth, 1, n))


def _norm_mod_kernel(x_ref, g_ref, mod_ref, o_ref):
    x = x_ref[0]
    y = x * lax.rsqrt(jnp.mean(x * x, axis=-1, keepdims=True) + RMS_EPS) * g_ref[...]
    shift = mod_ref[0, 0:1, :]
    scale = mod_ref[0, 1:2, :]
    o_ref[0] = (y * (1.0 + scale) + shift).astype(o_ref.dtype)


def norm_mod(x, g, mod3):
    b, l, d = x.shape
    tm = _tile(l, 256)
    return pl.pallas_call(
        _norm_mod_kernel,
        grid=(b, l // tm),
        in_specs=[pl.BlockSpec((1, tm, d), lambda i, t: (i, t, 0)),
                  pl.BlockSpec((1, d), lambda i, t: (0, 0)),
                  pl.BlockSpec((1, 3, d), lambda i, t: (i, 0, 0))],
        out_specs=pl.BlockSpec((1, tm, d), lambda i, t: (i, t, 0)),
        out_shape=jax.ShapeDtypeStruct((b, l, d), BF16),
        compiler_params=_cparams(("parallel", "parallel")),
        name="norm_mod",
    )(x, g.reshape(1, d), mod3)


def _final_norm_kernel(x_ref, g_ref, o_ref):
    x = x_ref[0]
    o_ref[0] = x * lax.rsqrt(jnp.mean(x * x, axis=-1, keepdims=True) + RMS_EPS) * g_ref[...]


def final_norm(x, g):
    b, l, d = x.shape
    tm = _tile(l, 256)
    return pl.pallas_call(
        _final_norm_kernel,
        grid=(b, l // tm),
        in_specs=[pl.BlockSpec((1, tm, d), lambda i, t: (i, t, 0)),
                  pl.BlockSpec((1, d), lambda i, t: (0, 0))],
        out_specs=pl.BlockSpec((1, tm, d), lambda i, t: (i, t, 0)),
        out_shape=jax.ShapeDtypeStruct((b, l, d), F32),
        compiler_params=_cparams(("parallel", "parallel")),
        name="final_norm",
    )(x, g.reshape(1, d))


def _mm_kernel(a_ref, w_ref, o_ref, wb_ref):
    @pl.when(pl.program_id(1) == 0)
    def _():
        wb_ref[...] = w_ref[0].astype(BF16)

    o_ref[...] = jnp.dot(a_ref[...], wb_ref[...], preferred_element_type=F32).astype(o_ref.dtype)


def matmul(a, w, layer, n_out, src_block, tn, out_dtype, name):
    m, k = a.shape
    tm = _tile(m, 1024)
    return pl.pallas_call(
        _mm_kernel,
        grid=(n_out // tn, m // tm),
        in_specs=[pl.BlockSpec((tm, k), lambda j, i: (i, 0)),
                  pl.BlockSpec((1, k, tn), lambda j, i: (layer, 0, src_block(j)))],
        out_specs=pl.BlockSpec((tm, tn), lambda j, i: (i, j)),
        out_shape=jax.ShapeDtypeStruct((m, n_out), out_dtype),
        scratch_shapes=[pltpu.VMEM((k, tn), BF16)],
        compiler_params=_cparams(("arbitrary", "arbitrary")),
        name=name,
    )(a, w)


def in_proj(h2d, w_in, layer, d):
    n_main = PM_G + N_BRANCH * d
    tn = _tile(n_main, 512)
    assert BRANCH_W % tn == 0
    per = BRANCH_W // tn
    ng = len(A_GROUPS)

    def main_src(j):
        return jnp.where(j < 3 * per, (j // per) * ng * per + j % per,
                         jnp.where(j < PM_CZ // tn, j + (OFF_AZ - PM_AZ) // tn, j + (OFF_CZ - PM_CZ) // tn))

    def side_src(j):
        c = j // per
        return jnp.where(j < SIDE_C // tn, ((c // 2) * ng + 1 + c % 2) * per + j % per, j + (OFF_C - SIDE_C) // tn)

    main = matmul(h2d, w_in, layer, n_main, main_src, tn, BF16, "in_proj")
    side = matmul(h2d, w_in, layer, SIDE_COLS, side_src, tn, F32, "in_proj_side")
    return main, side


A_UNROLL = 8
A_STAGE = 4


def _attn_a_kernel(slopes_ref, *refs, seq):
    qkv0 = refs[0:3]
    qkvd = refs[3:9]
    z_ref, o_ref = refs[9], refs[10]
    bias_sc, o_sc, lse_sc, k_st, v_st = refs[11:]
    h = pl.program_id(1)
    scale = HEAD_DIM ** -0.5
    ng = len(A_GROUPS)

    for g in list(range(1, ng)) + [0]:
        dil = A_GROUPS[g][1]
        n = seq // dil
        qb = min(A_QB, n)
        slab = min(qb + 2 * A_RADIUS, n)
        nqb = n // qb
        if dil == 1:
            q_ref, k_ref, v_ref = (r.at[0] for r in qkv0)
        else:
            q_ref, k_ref, v_ref = (qkvd[s * 2 + g - 1].at[0] for s in range(3))
        staged = dil > A_STAGE
        if staged:
            quarter = seq // A_STAGE
            for src, dst in ((k_ref, k_st), (v_ref, v_st)):
                for r4 in range(A_STAGE):
                    dst[pl.ds(r4 * quarter, quarter), :] = src[pl.ds(r4, quarter, stride=A_STAGE), :]
            k_ref, v_ref = k_st, v_st
        kv_stride = dil // A_STAGE if staged else dil
        slope = slopes_ref[g * A_HEADS + h] * float(dil)

        col_minus_row = (lax.broadcasted_iota(jnp.int32, (qb, slab), 1)
                         - lax.broadcasted_iota(jnp.int32, (qb, slab), 0))
        for cls, off in enumerate((0, -A_RADIUS, qb - slab)):
            dist = jnp.abs(col_minus_row + off)
            bias_sc[cls, 0:qb, 0:slab] = jnp.where(dist <= A_RADIUS, -slope * dist.astype(F32), NEG_INF)

        def q_rows(q0, r, dil=dil, qb=qb, q_ref=q_ref):
            if dil == 1:
                return q_ref[pl.ds(pl.multiple_of(q0, 64), qb), :]
            return q_ref[pl.ds(q0 * dil + r, qb, stride=dil), :].astype(BF16)

        def kv_rows(ref, ks, r, dil=dil, slab=slab, staged=staged, kv_stride=kv_stride):
            if dil == 1:
                return ref[pl.ds(pl.multiple_of(ks, 64), slab), :]
            if staged:
                base = (r % A_STAGE) * (seq // A_STAGE) + r // A_STAGE
            else:
                base = r
            return ref[pl.ds(base + ks * kv_stride, slab, stride=kv_stride), :].astype(BF16)

        def body(it, carry, g=g, dil=dil, n=n, qb=qb, slab=slab, nqb=nqb, q_rows=q_rows, kv_rows=kv_rows,
                 k_ref=k_ref, v_ref=v_ref):
            blks = [it * A_UNROLL + u for u in range(A_UNROLL)]
            res = [blk // nqb for blk in blks]
            q0s = [(blk % nqb) * qb for blk in blks]
            kss = [jnp.clip(q0 - A_RADIUS, 0, n - slab) for q0 in q0s]
            scores = [lax.dot_general(q_rows(q0, r), kv_rows(k_ref, ks, r),
                                      (((1,), (1,)), ((), ())), preferred_element_type=F32)
                      for q0, ks, r in zip(q0s, kss, res)]
            probs, stats = [], []
            for s, q0 in zip(scores, q0s):
                cls = jnp.where(q0 == 0, 0, jnp.where(q0 + qb == n, 2, 1))
                s = s * scale + bias_sc[cls, 0:qb, 0:slab]
                m = jnp.max(s, axis=-1, keepdims=True)
                p = jnp.exp(s - m)
                l = jnp.sum(p, axis=-1, keepdims=True)
                probs.append(p.astype(BF16))
                stats.append((m, l))
            outs = [jnp.dot(p, kv_rows(v_ref, ks, r), preferred_element_type=F32)
                    for p, ks, r in zip(probs, kss, res)]
            for o, (m, l), q0, r in zip(outs, stats, q0s, res):
                o = o * (1.0 / l)
                lse = jnp.broadcast_to(m + jnp.log(l), (qb, LANES))
                if dil != 1:
                    rows = pl.ds(q0 * dil + r, qb, stride=dil)
                    o_sc[g - 1, rows, :] = o
                    lse_sc[g - 1, rows, :] = lse
                else:
                    rows = pl.ds(pl.multiple_of(q0, 64), qb)
                    l1, l2 = lse_sc[0, rows, :], lse_sc[1, rows, :]
                    mx = jnp.maximum(jnp.maximum(lse, l1), l2)
                    e0, e1, e2 = jnp.exp(lse - mx), jnp.exp(l1 - mx), jnp.exp(l2 - mx)
                    y = (e0 * o + e1 * o_sc[0, rows, :] + e2 * o_sc[1, rows, :]) * (1.0 / (e0 + e1 + e2))
                    o_ref[0, rows, :] = (y * _silu(z_ref[0, rows, :].astype(F32))).astype(o_ref.dtype)
            return carry

        lax.fori_loop(0, (dil * nqb) // A_UNROLL, body, 0)


def attn_a(proj, proj_dil, slopes):
    b, l, _ = proj.shape
    ng = len(A_GROUPS)
    assert ng == 3 and (l // A_QB) % A_UNROLL == 0

    def spec(blk0):
        return pl.BlockSpec((1, l, HEAD_DIM), lambda i, h, blk0=blk0: (i, 0, blk0 + h))

    in_specs = [pl.BlockSpec(memory_space=pltpu.SMEM)]
    in_specs += [spec(s * A_HEADS) for s in range(3)]
    in_specs += [spec(c * A_HEADS) for c in range(3 * (ng - 1))]
    in_specs += [spec(PM_AZ // HEAD_DIM)]
    return pl.pallas_call(
        functools.partial(_attn_a_kernel, seq=l),
        grid=(b, A_HEADS),
        in_specs=in_specs,
        out_specs=pl.BlockSpec((1, l, HEAD_DIM), lambda i, h: (i, 0, h)),
        out_shape=jax.ShapeDtypeStruct((b, l, BRANCH_W), BF16),
        scratch_shapes=[pltpu.VMEM((3, A_QB, A_QB + 2 * A_RADIUS), F32),
                        pltpu.VMEM((ng - 1, l, LANES), F32), pltpu.VMEM((ng - 1, l, LANES), F32),
                        pltpu.VMEM((l, LANES), F32), pltpu.VMEM((l, LANES), F32)],
        compiler_params=_cparams(("parallel", "parallel")),
        name="attn_dilated",
    )(slopes, proj, proj, proj, *([proj_dil] * (3 * (ng - 1))), proj)


def _na_bias_tables(rpb, rows):
    kr = min(NA_KR, rows)
    qc = np.arange(GRID_W)
    cs = np.clip(qc - NA_KC // 2, 0, GRID_W - NA_KC)
    col_ok = (qc[None, :] >= cs[:, None]) & (qc[None, :] < cs[:, None] + NA_KC)
    dc = np.clip(qc[None, :] - qc[:, None] + NA_KC - 1, 0, 2 * NA_KC - 2)
    dc_onehot = (dc[..., None] == np.arange(2 * NA_KC - 1)).astype(np.float32)
    tabs = []
    for a in (0, 1, rows // NA_QROWS - 1):
        ks = int(np.clip(NA_QROWS * a - NA_QROWS, 0, rows - NA_KROWS))
        rq = NA_QROWS * a + np.arange(NA_QROWS)
        rs = np.clip(rq - kr // 2, 0, rows - kr)
        rk = ks + np.arange(NA_KROWS)
        row_ok = (rk[None, :] >= rs[:, None]) & (rk[None, :] < rs[:, None] + kr)
        dr = np.clip(rk[None, :] - rq[:, None] + NA_KR - 1, 0, 2 * NA_KR - 2)
        dr_onehot = (dr[..., None] == np.arange(2 * NA_KR - 1)).astype(np.float32)
        t = jnp.einsum("qjr,ckd,hrd->hqcjk", dr_onehot, dc_onehot, rpb.astype(F32),
                       precision=lax.Precision.HIGHEST)
        ok = row_ok[:, None, :, None] & col_ok[None, :, None, :]
        tabs.append(jnp.where(ok[None], t, NEG_INF).reshape(-1, NA_QT, NA_KT))
    return jnp.stack(tabs, axis=1)


NA_UNROLL = 2


def _attn_b_kernel(q_ref, k_ref, v_ref, z_ref, bias_ref, o_ref, *, seq):
    rows = seq // GRID_W
    nblk = rows // NA_QROWS
    scale = HEAD_DIM ** -0.5

    def body(it, carry):
        blks = [it * NA_UNROLL + u for u in range(NA_UNROLL)]
        ks_rows = [jnp.clip(NA_QROWS * a - NA_QROWS, 0, rows - NA_KROWS) for a in blks]
        qrows = [pl.ds(pl.multiple_of(a * NA_QT, NA_QT), NA_QT) for a in blks]
        krows = [pl.ds(pl.multiple_of(ks * GRID_W, GRID_W), NA_KT) for ks in ks_rows]
        scores = [lax.dot_general(q_ref[0, qr, :], k_ref[0, kr, :], (((1,), (1,)), ((), ())),
                                  preferred_element_type=F32) for qr, kr in zip(qrows, krows)]
        probs, sums = [], []
        for s, a, ks in zip(scores, blks, ks_rows):
            s = s * scale + bias_ref[0, (NA_QROWS * a - ks) // NA_QROWS]
            p = jnp.exp(s - jnp.max(s, axis=-1, keepdims=True))
            sums.append(jnp.sum(p, axis=-1, keepdims=True))
            probs.append(p.astype(BF16))
        outs = [jnp.dot(p, v_ref[0, kr, :], preferred_element_type=F32) for p, kr in zip(probs, krows)]
        for o, l, qr in zip(outs, sums, qrows):
            o_ref[0, qr, :] = (o * (1.0 / l) * _silu(z_ref[0, qr, :].astype(F32))).astype(o_ref.dtype)
        return carry

    lax.fori_loop(0, nblk // NA_UNROLL, body, 0)


def attn_b(proj, bias):
    b, l, _ = proj.shape
    assert l % (NA_QT * NA_UNROLL) == 0 and l // GRID_W >= NA_KROWS

    def col_spec(off):
        return pl.BlockSpec((1, l, HEAD_DIM), lambda h, i, off=off: (i, 0, off // HEAD_DIM + h))

    return pl.pallas_call(
        functools.partial(_attn_b_kernel, seq=l),
        grid=(B_HEADS, b),
        in_specs=[col_spec(PM_B), col_spec(PM_B + B_HEADS * HEAD_DIM),
                  col_spec(PM_B + 2 * B_HEADS * HEAD_DIM), col_spec(PM_BZ),
                  pl.BlockSpec((1, 3, NA_QT, NA_KT), lambda h, i: (h, 0, 0, 0))],
        out_specs=pl.BlockSpec((1, l, HEAD_DIM), lambda h, i: (i, 0, h)),
        out_shape=jax.ShapeDtypeStruct((b, l, BRANCH_W), BF16),
        compiler_params=_cparams(("parallel", "parallel")),
        name="attn_neighbourhood",
    )(proj, proj, proj, proj, bias)


HY_TAB = 6
HY_ROW0 = 8


def _shift_down(x):
    t = lax.broadcasted_iota(jnp.int32, x.shape, 0)
    return jnp.where(t == 0, 0.0, pltpu.roll(x, 1, axis=0))


def _shift_up(x):
    n = x.shape[0]
    t = lax.broadcasted_iota(jnp.int32, x.shape, 0)
    return jnp.where(t == n - 1, 0.0, pltpu.roll(x, n - 1, axis=0))


def _conv3_kernel(u_ref, w_ref, b_ref, o_ref, *, half):
    e = u_ref[0, pl.ds(0, half, stride=2), :]
    o = u_ref[0, pl.ds(1, half, stride=2), :]
    w0, w1, w2, bias = w_ref[0:1, :], w_ref[1:2, :], w_ref[2:3, :], b_ref[...]
    o_ref[0, 0] = (w0 * _shift_down(o) + w1 * e + w2 * o + bias).astype(o_ref.dtype)
    o_ref[0, 1] = (w0 * e + w1 * o + w2 * _shift_up(e) + bias).astype(o_ref.dtype)


def conv3(side, conv_w, conv_b):
    b, l, _ = side.shape
    half = l // 2
    return pl.pallas_call(
        functools.partial(_conv3_kernel, half=half),
        grid=(b, C_PROJ // LANES),
        in_specs=[pl.BlockSpec((1, l, LANES), lambda i, j: (i, 0, SIDE_C // LANES + j)),
                  pl.BlockSpec((3, LANES), lambda i, j: (0, j)),
                  pl.BlockSpec((1, LANES), lambda i, j: (0, j))],
        out_specs=pl.BlockSpec((1, 2, half, LANES), lambda i, j: (i, 0, 0, j)),
        out_shape=jax.ShapeDtypeStruct((b, 2, half, C_PROJ), BF16),
        compiler_params=_cparams(("parallel", "parallel")),
        name="hyena_conv3",
    )(side, conv_w, conv_b.reshape(1, C_PROJ))


def _filter_kernel(feats_ref, w1_ref, b1_ref, w2_ref, b2_ref, freq_ref, w3_ref, dec_ref, o_ref, hid_ref,
                   *, tc):
    j = pl.program_id(0)
    hp = lax.Precision.HIGHEST

    @pl.when(j == 0)
    def _():
        h1 = jnp.sin(freq_ref[0:1, :] * (jnp.dot(feats_ref[...], w1_ref[...], precision=hp,
                                                 preferred_element_type=F32) + b1_ref[...]))
        hid_ref[...] = jnp.sin(freq_ref[1:2, :] * (jnp.dot(h1, w2_ref[...], precision=hp,
                                                           preferred_element_type=F32) + b2_ref[...]))

    raw = jnp.dot(hid_ref[...].astype(BF16), w3_ref[...].astype(BF16), preferred_element_type=F32)
    t_norm = feats_ref[:, 0:1]
    hfil = raw * jnp.exp(-t_norm * jnp.exp(dec_ref[...]))
    hfil = hfil * (1.0 / (jnp.sum(jnp.abs(hfil), axis=0, keepdims=True) + 1e-6))
    dropped_row = ((j * tc) // HY_CH) % 2 - 1
    t = lax.broadcasted_iota(jnp.int32, hfil.shape, 0)
    o_ref[...] = jnp.where(t == dropped_row, 0.0, hfil).astype(o_ref.dtype)


def hyena_filter_taps(seq, w1, b1, w2, b2, w3, freq, log_decay):
    t = jnp.concatenate([jnp.arange(0, seq, 2), jnp.arange(1, seq, 2)]).astype(F32)
    t_norm = t / (seq - 1)
    ang = (2.0 * math.pi / seq) * t
    bands = jnp.linspace(1e-4, HY_BANDS - 1, HY_BANDS, dtype=F32)
    feats = jnp.concatenate([t_norm[:, None], jnp.cos(ang[:, None] * bands), -jnp.sin(ang[:, None] * bands)], axis=-1)
    emb = 64
    feats = jnp.pad(feats, ((0, 0), (0, emb - HY_EMB)))
    w1p = jnp.pad(w1, ((0, emb - HY_EMB), (0, 0)))
    ncol = HY_ORDER * 2 * HY_CH
    tc = 256
    full = lambda shape: pl.BlockSpec(shape, lambda j: (0, 0))
    return pl.pallas_call(
        functools.partial(_filter_kernel, tc=tc),
        grid=(ncol // tc,),
        in_specs=[full((seq, emb)), full((emb, HY_FFN)), full((1, HY_FFN)), full((HY_FFN, HY_FFN)),
                  full((1, HY_FFN)), full((2, HY_FFN)),
                  pl.BlockSpec((HY_FFN, tc), lambda j: (0, j)),
                  pl.BlockSpec((1, tc), lambda j: (0, j))],
        out_specs=pl.BlockSpec((seq, tc), lambda j: (0, j)),
        out_shape=jax.ShapeDtypeStruct((seq, ncol), BF16),
        scratch_shapes=[pltpu.VMEM((seq, HY_FFN), F32)],
        compiler_params=_cparams(("arbitrary",)),
        name="hyena_filter_taps",
    )(feats, w1p, b1.reshape(1, HY_FFN), w2, b2.reshape(1, HY_FFN), freq, w3, log_decay.reshape(1, ncol))


def _dft_consts(seq):
    half = seq // 2
    k = jnp.arange(half, dtype=jnp.int32)
    ang = ((k[:, None] * k[None, :]) % seq).astype(F32) * (2.0 * math.pi / seq)
    cmat = jnp.cos(ang)
    smat = jnp.sin(ang)
    sign = (1 - 2 * (k % 2)).astype(F32)
    smat = jnp.where((k == 0)[:, None], sign[None, :], smat)
    tw = k.astype(F32) * (2.0 * math.pi / seq)
    c2 = jnp.broadcast_to(jnp.cos(tw)[:, None], (half, LANES))
    s2 = jnp.broadcast_to(jnp.sin(tw)[:, None], (half, LANES))
    return dict(cmat=cmat.astype(BF16), smat=smat.astype(BF16), smat_t=smat.T.astype(BF16), c2=c2, s2=s2)


def _half_transforms(c_ref, s_ref, x_ref):
    cm, sm = c_ref[...], s_ref[...]
    e, o = x_ref[0], x_ref[1]
    dot = lambda a, b: jnp.dot(a, b, preferred_element_type=F32)
    return dot(cm, e), dot(sm, e), dot(cm, o), dot(sm, o)


def _filter_spec_kernel(c_ref, s_ref, c2_ref, s2_ref, xf_ref, xb_ref, tab_ref, row0_ref, *, seq, tf, tc):
    i = pl.program_id(2)
    pef, qef, pof, qof = _half_transforms(c_ref, s_ref, xf_ref)
    peb, qeb, pob, qob = _half_transforms(c_ref, s_ref, xb_ref)
    c2 = jnp.tile(c2_ref[...], (1, tc // LANES))
    s2 = jnp.tile(s2_ref[...], (1, tc // LANES))
    sc = 2.0 * (2.0 / (2 * seq))
    tab_ref[0, 0] = sc * (pef + peb)
    tab_ref[0, 1] = sc * (qeb - qef)
    tab_ref[0, 2] = sc * (c2 * pof - s2 * qof + pob)
    tab_ref[0, 3] = sc * (qob - c2 * qof - s2 * pof)
    tab_ref[0, 4] = sc * (pof + c2 * pob - s2 * qob)
    tab_ref[0, 5] = sc * (c2 * qob + s2 * pob - qof)

    @pl.when(i == 0)
    def _():
        inv_n = 1.0 / (2 * seq)
        row0_ref[0] = jnp.zeros(row0_ref.shape[1:], F32)
        row0_ref[0, 0:1, :] = 2.0 * inv_n * (pef[0:1] + peb[0:1])
        row0_ref[0, 1:2, :] = 2.0 * inv_n * (pof[0:1] + pob[0:1])
        row0_ref[0, 2:3, :] = 2.0 * inv_n * (qef[0:1] + qeb[0:1])
        row0_ref[0, 3:4, :] = 2.0 * inv_n * (qob[0:1] - qof[0:1])


def hyena_filter_spectrum(taps, dc):
    seq = taps.shape[0]
    half = seq // 2
    taps = taps.reshape(2, half, taps.shape[1])
    tf, tc = _tile(half, 256), 512
    ncb = HY_CH // tc
    mspec = pl.BlockSpec((tf, half), lambda o, j, i: (i, 0))
    tspec = pl.BlockSpec((tf, LANES), lambda o, j, i: (i, 0))
    return pl.pallas_call(
        functools.partial(_filter_spec_kernel, seq=seq, tf=tf, tc=tc),
        grid=(HY_ORDER, ncb, half // tf),
        in_specs=[mspec, mspec, tspec, tspec,
                  pl.BlockSpec((2, half, tc), lambda o, j, i: (0, 0, (o * 2) * ncb + j)),
                  pl.BlockSpec((2, half, tc), lambda o, j, i: (0, 0, (o * 2 + 1) * ncb + j))],
        out_specs=[pl.BlockSpec((1, HY_TAB, tf, tc), lambda o, j, i: (o, 0, i, j)),
                   pl.BlockSpec((1, HY_ROW0, tc), lambda o, j, i: (o, 0, j))],
        out_shape=[jax.ShapeDtypeStruct((HY_ORDER, HY_TAB, half, HY_CH), F32),
                   jax.ShapeDtypeStruct((HY_ORDER, HY_ROW0, HY_CH), F32)],
        compiler_params=_cparams(("parallel", "parallel", "arbitrary")),
        name="hyena_filter_spectrum",
    )(dc["cmat"], dc["smat"], dc["c2"], dc["s2"], taps, taps)


def _conv_fwd_kernel(c_ref, s_ref, z_ref, tab_ref, row0_ref, uv_ref, *, tf):
    i = pl.program_id(1)
    pe, qe, po, qo = _half_transforms(c_ref, s_ref, z_ref.at[0])
    spr, spi, wmr, wmi, vmr, vmi = (tab_ref[0, n] for n in range(HY_TAB))
    u0 = pe * spr + qe * spi + po * wmr + qo * wmi
    v0 = qe * spr - pe * spi + qo * wmr - po * wmi
    u1 = pe * vmr + qe * vmi + po * spr + qo * spi
    v1 = qe * vmr - pe * vmi + qo * spr - po * spi
    ra, rb, rc, rd = (row0_ref[0, n:n + 1, :] for n in range(4))
    first = (lax.broadcasted_iota(jnp.int32, u0.shape, 0) + i * tf) == 0
    pe0, qe0, po0, qo0 = pe[0:1], qe[0:1], po[0:1], qo[0:1]
    uv_ref[0, 0] = jnp.where(first, pe0 * ra + po0 * rb, u0).astype(uv_ref.dtype)
    uv_ref[0, 1] = jnp.where(first, qe0 * rc + qo0 * rd, v0).astype(uv_ref.dtype)
    uv_ref[0, 2] = jnp.where(first, pe0 * rb + po0 * ra, u1).astype(uv_ref.dtype)
    uv_ref[0, 3] = jnp.where(first, qo0 * rc - qe0 * rd, v1).astype(uv_ref.dtype)


def conv_fwd(z, zcol, dc, tabs, row0, order):
    b, _, half, _ = z.shape
    tf, tc = _tile(half, 256), 512
    ncb = HY_CH // tc
    mspec = pl.BlockSpec((tf, half), lambda j, i, n: (i, 0))
    return pl.pallas_call(
        functools.partial(_conv_fwd_kernel, tf=tf),
        grid=(ncb, half // tf, b),
        in_specs=[mspec, mspec,
                  pl.BlockSpec((1, 2, half, tc), lambda j, i, n: (n, 0, 0, zcol // tc + j)),
                  pl.BlockSpec((1, HY_TAB, tf, tc), lambda j, i, n: (order, 0, i, j)),
                  pl.BlockSpec((1, HY_ROW0, tc), lambda j, i, n: (order, 0, j))],
        out_specs=pl.BlockSpec((1, 4, tf, tc), lambda j, i, n: (n, 0, i, j)),
        out_shape=jax.ShapeDtypeStruct((b, 4, half, HY_CH), BF16),
        compiler_params=_cparams(("parallel", "parallel", "arbitrary")),
        name="hyena_conv_fwd",
    )(dc["cmat"], dc["smat"], z, tabs, row0)


def _conv_inv_kernel(c_ref, st_ref, uv_ref, x_ref, zp_ref, skip_ref, *rest, gated, tt, tc):
    cm, sm = c_ref[...], st_ref[...]
    dot = lambda a, b: jnp.dot(a, b, preferred_element_type=F32)
    skip = skip_ref[0]
    halves = []
    for par in range(2):
        y = dot(cm, uv_ref[0, 2 * par]) + dot(sm, uv_ref[0, 2 * par + 1])
        halves.append(x_ref[0, par].astype(F32) * (y + zp_ref[0, par].astype(F32) * skip))
    if not gated:
        o_ref = rest[0]
        o_ref[0, 0] = halves[0].astype(o_ref.dtype)
        o_ref[0, 1] = halves[1].astype(o_ref.dtype)
        return
    gate_ref, o_ref, il_ref = rest
    for s in range(tc // LANES):
        lanes = slice(s * LANES, (s + 1) * LANES)
        il_ref[s, pl.ds(0, tt, stride=2), :] = halves[0][:, lanes]
        il_ref[s, pl.ds(1, tt, stride=2), :] = halves[1][:, lanes]
    for s in range(tc // LANES):
        lanes = slice(s * LANES, (s + 1) * LANES)
        o_ref[0, :, lanes] = (il_ref[s] * _silu(gate_ref[0, :, lanes].astype(F32))).astype(o_ref.dtype)


def conv_inv(uv, dc, xmul, xcol, zprev, zcol, skip, gate=None, gcol=0):
    b, _, half, _ = uv.shape
    tt, tc = _tile(half, 256), 512
    ncb = HY_CH // tc
    mspec = pl.BlockSpec((tt, half), lambda n, j, i: (i, 0))

    def tile_spec(col):
        return pl.BlockSpec((1, 2, tt, tc), lambda n, j, i, col=col: (n, 0, i, col // tc + j))

    in_specs = [mspec, mspec, pl.BlockSpec((1, 4, half, tc), lambda n, j, i: (n, 0, 0, j)),
                tile_spec(xcol), tile_spec(zcol), pl.BlockSpec((1, 1, tc), lambda n, j, i: (0, 0, j))]
    args = [dc["cmat"], dc["smat_t"], uv, xmul, zprev, skip.reshape(1, 1, HY_CH)]
    scratch = []
    if gate is None:
        out_spec = pl.BlockSpec((1, 2, tt, tc), lambda n, j, i: (n, 0, i, j))
        out_shape = jax.ShapeDtypeStruct((b, 2, half, HY_CH), BF16)
    else:
        in_specs.append(pl.BlockSpec((1, 2 * tt, tc), lambda n, j, i: (n, i, gcol // tc + j)))
        args.append(gate)
        out_spec = pl.BlockSpec((1, 2 * tt, tc), lambda n, j, i: (n, i, j))
        out_shape = jax.ShapeDtypeStruct((b, 2 * half, HY_CH), BF16)
        scratch = [pltpu.VMEM((tc // LANES, 2 * tt, LANES), F32)]
    return pl.pallas_call(
        functools.partial(_conv_inv_kernel, gated=gate is not None, tt=tt, tc=tc),
        grid=(b, ncb, half // tt),
        in_specs=in_specs,
        out_specs=out_spec,
        out_shape=out_shape,
        scratch_shapes=scratch,
        compiler_params=_cparams(("parallel", "parallel", "arbitrary")),
        name="hyena_conv_inv",
    )(*args)


def mixer_hyena(proj, side, dc, tabs, row0, conv_w, conv_b, skip):
    u = conv3(side, conv_w, conv_b)
    uv = conv_fwd(u, 0, dc, tabs, row0, 0)
    z1 = conv_inv(uv, dc, u, HY_CH, u, 0, skip[0])
    uv = conv_fwd(z1, 0, dc, tabs, row0, 1)
    return conv_inv(uv, dc, u, 2 * HY_CH, z1, 0, skip[1], gate=proj, gcol=PM_CZ)


def _merge_kernel(ya_ref, yb_ref, yc_ref, w_ref, g0_ref, g1_ref, g2_ref, o_ref):
    acc = None
    for y_ref, g_ref, n in ((ya_ref, g0_ref, 0), (yb_ref, g1_ref, 1), (yc_ref, g2_ref, 2)):
        t = _sigmoid(g_ref[...].astype(F32)) * jnp.dot(y_ref[...], w_ref[n], preferred_element_type=F32)
        acc = t if acc is None else acc + t
    o_ref[...] = acc.astype(o_ref.dtype)


def merge_branches(ya, yb, yc, w_branch, proj2d, d):
    m = ya.shape[0]
    tm, tn = _tile(m, 1024), _tile(d, 512)
    yspec = pl.BlockSpec((tm, BRANCH_W), lambda i, j: (i, 0))

    def gspec(n):
        return pl.BlockSpec((tm, tn), lambda i, j, n=n: (i, (PM_G + n * d) // tn + j))

    return pl.pallas_call(
        _merge_kernel,
        grid=(m // tm, d // tn),
        in_specs=[yspec, yspec, yspec,
                  pl.BlockSpec((N_BRANCH, BRANCH_W, tn), lambda i, j: (0, 0, j)),
                  gspec(0), gspec(1), gspec(2)],
        out_specs=pl.BlockSpec((tm, tn), lambda i, j: (i, j)),
        out_shape=jax.ShapeDtypeStruct((m, d), BF16),
        compiler_params=_cparams(("parallel", "arbitrary")),
        name="branch_merge",
    )(ya, yb, yc, w_branch, proj2d, proj2d, proj2d)


def _out_kernel(a_ref, w_ref, x_ref, mod_ref, o_ref):
    acc = jnp.dot(a_ref[0], w_ref[...], preferred_element_type=F32)
    o_ref[0] = x_ref[0] + mod_ref[0, 2:3, :] * acc


def out_proj(mixed, w_out, x, mod3):
    b, l, d = x.shape
    tm, tn = _tile(l, 1024), _tile(d, 512)
    return pl.pallas_call(
        _out_kernel,
        grid=(b, l // tm, d // tn),
        in_specs=[pl.BlockSpec((1, tm, d), lambda n, i, j: (n, i, 0)),
                  pl.BlockSpec((d, tn), lambda n, i, j: (0, j)),
                  pl.BlockSpec((1, tm, tn), lambda n, i, j: (n, i, j)),
                  pl.BlockSpec((1, 3, tn), lambda n, i, j: (n, 0, j))],
        out_specs=pl.BlockSpec((1, tm, tn), lambda n, i, j: (n, i, j)),
        out_shape=jax.ShapeDtypeStruct((b, l, d), F32),
        compiler_params=_cparams(("parallel", "parallel", "arbitrary")),
        name="out_proj",
    )(mixed, w_out, x, mod3)


def _trunk(x, mods, layers, w_in, dc, slopes, final_g):
    b, l, d = x.shape
    for layer, (lw, mod3) in enumerate(zip(layers, mods)):
        h = norm_mod(x, lw["norm_g"], mod3)
        proj2d, side2d = in_proj(h.reshape(b * l, d), w_in, layer, d)
        proj = proj2d.reshape(b, l, -1)
        side = side2d.reshape(b, l, -1)
        ya = attn_a(proj, side, slopes)
        yb = attn_b(proj, _na_bias_tables(lw["na_rpb"], l // GRID_W))
        taps = hyena_filter_taps(l, lw["hy_w1"], lw["hy_b1"], lw["hy_w2"], lw["hy_b2"], lw["hy_w3"],
                                 lw["hy_freq"], lw["hy_log_decay"])
        tabs, row0 = hyena_filter_spectrum(taps, dc)
        yc = mixer_hyena(proj, side, dc, tabs, row0, lw["conv_w"], lw["conv_b"], lw["hy_skip"])
        mixed = merge_branches(ya.reshape(b * l, -1), yb.reshape(b * l, -1), yc.reshape(b * l, -1),
                               lw["w_branch"], proj2d, d)
        x = out_proj(mixed.reshape(b, l, d), lw["w_out"], x, mod3)
    return final_norm(x, final_g)


def kernel(x_prompt, x_sample, c_prompt, c_sample, norm_g, w_ada, b_ada, w_in, w_branch, w_out, na_rpb, conv_w, conv_b, hy_w1, hy_b1, hy_w2, hy_b2, hy_w3, hy_freq, hy_log_decay, hy_skip, final_g):
    depth, d = norm_g.shape
    nbp = c_prompt.shape[0]
    c_all = jnp.concatenate([c_prompt, c_sample], axis=0)
    nslope = len(A_GROUPS) * A_HEADS
    slopes = jnp.asarray(2.0 ** (-8.0 * (np.arange(nslope) + 1) / nslope), F32)
    layers, mods_p, mods_s = [], [], []
    for i in range(depth):
        mod3 = ada_mod(c_all, w_ada, b_ada, i).reshape(c_all.shape[0], 3, d)
        mods_p.append(mod3[:nbp])
        mods_s.append(mod3[nbp:])
        layers.append(dict(
            norm_g=norm_g[i], w_branch=w_branch[i].astype(BF16),
            w_out=w_out[i].astype(BF16), na_rpb=na_rpb[i], conv_w=conv_w[i], conv_b=conv_b[i],
            hy_w1=hy_w1[i], hy_b1=hy_b1[i], hy_w2=hy_w2[i], hy_b2=hy_b2[i], hy_w3=hy_w3[i],
            hy_freq=hy_freq[i], hy_log_decay=hy_log_decay[i], hy_skip=hy_skip[i]))
    outs = []
    for x, mods in ((x_prompt, mods_p), (x_sample, mods_s)):
        outs.append(_trunk(x, mods, layers, w_in, _dft_consts(x.shape[1]), slopes, final_g))
    return tuple(outs)
```

```python
import functools
import math

import jax
import jax.numpy as jnp
import numpy as np
from jax import lax
from jax.experimental import pallas as pl
from jax.experimental.pallas import tpu as pltpu

F32 = jnp.float32
BF16 = jnp.bfloat16

LANES = 128
V7X_VMEM_LIMIT = 56 * 1024 * 1024

HEAD_DIM = 128
GRID_W = 64
BRANCH_W = 1024
N_BRANCH = 3
A_GROUPS = ((128, 1), (512, 4), (2048, 16))
A_HEADS = 8
A_RADIUS = 64
B_HEADS = 8
NA_KR = 8
NA_KC = 16
HY_ORDER = 2
HY_CH = BRANCH_W
HY_BANDS = 16
HY_EMB = 1 + 2 * HY_BANDS
HY_FFN = 64
RMS_EPS = 1e-6
NEG_INF = -1e30

A_QKV = 3 * len(A_GROUPS) * A_HEADS * HEAD_DIM
B_QKV = 3 * B_HEADS * HEAD_DIM
C_PROJ = (HY_ORDER + 1) * HY_CH
OFF_AZ = A_QKV
OFF_B = OFF_AZ + BRANCH_W
OFF_BZ = OFF_B + B_QKV
OFF_C = OFF_BZ + BRANCH_W
OFF_CZ = OFF_C + C_PROJ
OFF_G = OFF_CZ + BRANCH_W

PM_AZ = 3 * BRANCH_W
PM_B = PM_AZ + BRANCH_W
PM_BZ = PM_B + B_QKV
PM_CZ = PM_BZ + BRANCH_W
PM_G = PM_CZ + BRANCH_W
SIDE_C = 3 * (len(A_GROUPS) - 1) * BRANCH_W
SIDE_COLS = SIDE_C + C_PROJ

NA_QROWS = 4
NA_KROWS = 12
NA_QT = NA_QROWS * GRID_W
NA_KT = NA_KROWS * GRID_W

A_QB = 128


def _cparams(sem):
    return pltpu.CompilerParams(dimension_semantics=sem, vmem_limit_bytes=V7X_VMEM_LIMIT)


def _tile(n, pref):
    t = min(n, pref)
    while n % t:
        t //= 2
    return t


def _silu(z):
    return z * (1.0 / (1.0 + jnp.exp(-z)))


def _sigmoid(z):
    return 1.0 / (1.0 + jnp.exp(-z))


def _ada_kernel(c_ref, w_ref, b_ref, o_ref):
    a = _silu(c_ref[...]).astype(BF16)
    o_ref[...] = jnp.dot(a, w_ref[0].astype(BF16), preferred_element_type=F32) + b_ref[0]


def ada_mod(c, w_ada, b_ada, layer):
    nb, d = c.shape
    depth, _, n = w_ada.shape
    tn = _tile(n, 512)
    return pl.pallas_call(
        _ada_kernel,
        grid=(n // tn,),
        in_specs=[pl.BlockSpec((nb, d), lambda j: (0, 0)),
                  pl.BlockSpec((1, d, tn), lambda j: (layer, 0, j)),
                  pl.BlockSpec((1, 1, tn), lambda j: (layer, 0, j))],
        out_specs=pl.BlockSpec((nb, tn), lambda j: (0, j)),
        out_shape=jax.ShapeDtypeStruct((nb, n), F32),
        compiler_params=_cparams(("arbitrary",)),
        name="ada_mod",
    )(c, w_ada, b_ada.reshape(depth, 1, n))


def _norm_mod_kernel(x_ref, g_ref, mod_ref, o_ref):
    x = x_ref[0]
    y = x * lax.rsqrt(jnp.mean(x * x, axis=-1, keepdims=True) + RMS_EPS) * g_ref[...]
    shift = mod_ref[0, 0:1, :]
    scale = mod_ref[0, 1:2, :]
    o_ref[0] = (y * (1.0 + scale) + shift).astype(o_ref.dtype)


def norm_mod(x, g, mod3):
    b, l, d = x.shape
    tm = _tile(l, 256)
    return pl.pallas_call(
        _norm_mod_kernel,
        grid=(b, l // tm),
        in_specs=[pl.BlockSpec((1, tm, d), lambda i, t: (i, t, 0)),
                  pl.BlockSpec((1, d), lambda i, t: (0, 0)),
                  pl.BlockSpec((1, 3, d), lambda i, t: (i, 0, 0))],
        out_specs=pl.BlockSpec((1, tm, d), lambda i, t: (i, t, 0)),
        out_shape=jax.ShapeDtypeStruct((b, l, d), BF16),
        compiler_params=_cparams(("parallel", "parallel")),
        name="norm_mod",
    )(x, g.reshape(1, d), mod3)


def _final_norm_kernel(x_ref, g_ref, o_ref):
    x = x_ref[0]
    o_ref[0] = x * lax.rsqrt(jnp.mean(x * x, axis=-1, keepdims=True) + RMS_EPS) * g_ref[...]


def final_norm(x, g):
    b, l, d = x.shape
    tm = _tile(l, 256)
    return pl.pallas_call(
        _final_norm_kernel,
        grid=(b, l // tm),
        in_specs=[pl.BlockSpec((1, tm, d), lambda i, t: (i, t, 0)),
                  pl.BlockSpec((1, d), lambda i, t: (0, 0))],
        out_specs=pl.BlockSpec((1, tm, d), lambda i, t: (i, t, 0)),
        out_shape=jax.ShapeDtypeStruct((b, l, d), F32),
        compiler_params=_cparams(("parallel", "parallel")),
        name="final_norm",
    )(x, g.reshape(1, d))


def _mm_kernel(a_ref, w_ref, o_ref, wb_ref):
    @pl.when(pl.program_id(1) == 0)
    def _():
        wb_ref[...] = w_ref[0].astype(BF16)

    o_ref[...] = jnp.dot(a_ref[...], wb_ref[...], preferred_element_type=F32).astype(o_ref.dtype)


def matmul(a, w, layer, n_out, src_block, tn, out_dtype, name):
    m, k = a.shape
    tm = _tile(m, 1024)
    nrow = m // tm

    def row(j, i):
        return jnp.where(j % 2 == 0, i, nrow - 1 - i)

    return pl.pallas_call(
        _mm_kernel,
        grid=(n_out // tn, nrow),
        in_specs=[pl.BlockSpec((tm, k), lambda j, i: (row(j, i), 0)),
                  pl.BlockSpec((1, k, tn), lambda j, i: (layer, 0, src_block(j)))],
        out_specs=pl.BlockSpec((tm, tn), lambda j, i: (row(j, i), j)),
        out_shape=jax.ShapeDtypeStruct((m, n_out), out_dtype),
        scratch_shapes=[pltpu.VMEM((k, tn), BF16)],
        compiler_params=_cparams(("arbitrary", "arbitrary")),
        name=name,
    )(a, w)


def in_proj(h2d, w_in, layer, d):
    n_main = PM_G + N_BRANCH * d
    tn = _tile(n_main, 512)
    assert BRANCH_W % tn == 0
    per = BRANCH_W // tn
    ng = len(A_GROUPS)

    def main_src(j):
        return jnp.where(j < 3 * per, (j // per) * ng * per + j % per,
                         jnp.where(j < PM_CZ // tn, j + (OFF_AZ - PM_AZ) // tn, j + (OFF_CZ - PM_CZ) // tn))

    def side_src(j):
        c = j // per
        return jnp.where(j < SIDE_C // tn, ((c // 2) * ng + 1 + c % 2) * per + j % per, j + (OFF_C - SIDE_C) // tn)

    main = matmul(h2d, w_in, layer, n_main, main_src, tn, BF16, "in_proj")
    side = matmul(h2d, w_in, layer, SIDE_COLS, side_src, tn, F32, "in_proj_side")
    return main, side


A_UNROLL = 8
A_STAGE = 4


def _attn_a_kernel(slopes_ref, *refs, seq):
    qkv0 = refs[0:3]
    qkvd = refs[3:9]
    z_ref, o_ref = refs[9], refs[10]
    bias_sc, o_sc, lse_sc, k_st, v_st = refs[11:]
    h = pl.program_id(1)
    scale = HEAD_DIM ** -0.5
    ng = len(A_GROUPS)

    for g in list(range(1, ng)) + [0]:
        dil = A_GROUPS[g][1]
        n = seq // dil
        qb = min(A_QB, n)
        slab = min(qb + 2 * A_RADIUS, n)
        nqb = n // qb
        if dil == 1:
            q_ref, k_ref, v_ref = (r.at[0] for r in qkv0)
        else:
            q_ref, k_ref, v_ref = (qkvd[s * 2 + g - 1].at[0] for s in range(3))
        staged = dil > A_STAGE
        if staged:
            quarter = seq // A_STAGE
            for src, dst in ((k_ref, k_st), (v_ref, v_st)):
                for r4 in range(A_STAGE):
                    dst[pl.ds(r4 * quarter, quarter), :] = src[pl.ds(r4, quarter, stride=A_STAGE), :]
            k_ref, v_ref = k_st, v_st
        kv_stride = dil // A_STAGE if staged else dil
        slope = slopes_ref[g * A_HEADS + h] * float(dil)

        col_minus_row = (lax.broadcasted_iota(jnp.int32, (qb, slab), 1)
                         - lax.broadcasted_iota(jnp.int32, (qb, slab), 0))
        for cls, off in enumerate((0, -A_RADIUS, qb - slab)):
            dist = jnp.abs(col_minus_row + off)
            bias_sc[cls, 0:qb, 0:slab] = jnp.where(dist <= A_RADIUS, -slope * dist.astype(F32), NEG_INF)

        def q_rows(q0, r, dil=dil, qb=qb, q_ref=q_ref):
            if dil == 1:
                return q_ref[pl.ds(pl.multiple_of(q0, 64), qb), :]
            return q_ref[pl.ds(q0 * dil + r, qb, stride=dil), :].astype(BF16)

        def kv_rows(ref, ks, r, dil=dil, slab=slab, staged=staged, kv_stride=kv_stride):
            if dil == 1:
                return ref[pl.ds(pl.multiple_of(ks, 64), slab), :]
            if staged:
                base = (r % A_STAGE) * (seq // A_STAGE) + r // A_STAGE
            else:
                base = r
            return ref[pl.ds(base + ks * kv_stride, slab, stride=kv_stride), :].astype(BF16)

        def body(it, carry, g=g, dil=dil, n=n, qb=qb, slab=slab, nqb=nqb, q_rows=q_rows, kv_rows=kv_rows,
                 k_ref=k_ref, v_ref=v_ref):
            blks = [it * A_UNROLL + u for u in range(A_UNROLL)]
            res = [blk // nqb for blk in blks]
            q0s = [(blk % nqb) * qb for blk in blks]
            kss = [jnp.clip(q0 - A_RADIUS, 0, n - slab) for q0 in q0s]
            scores = [lax.dot_general(q_rows(q0, r), kv_rows(k_ref, ks, r),
                                      (((1,), (1,)), ((), ())), preferred_element_type=F32)
                      for q0, ks, r in zip(q0s, kss, res)]
            probs, stats = [], []
            for s, q0 in zip(scores, q0s):
                cls = jnp.where(q0 == 0, 0, jnp.where(q0 + qb == n, 2, 1))
                s = s * scale + bias_sc[cls, 0:qb, 0:slab]
                m = jnp.max(s, axis=-1, keepdims=True)
                p = jnp.exp(s - m)
                l = jnp.sum(p, axis=-1, keepdims=True)
                probs.append(p.astype(BF16))
                stats.append((m, l))
            outs = [jnp.dot(p, kv_rows(v_ref, ks, r), preferred_element_type=F32)
                    for p, ks, r in zip(probs, kss, res)]
            for o, (m, l), q0, r in zip(outs, stats, q0s, res):
                o = o * (1.0 / l)
                lse = jnp.broadcast_to(m + jnp.log(l), (qb, LANES))
                if dil != 1:
                    rows = pl.ds(q0 * dil + r, qb, stride=dil)
                    o_sc[g - 1, rows, :] = o
                    lse_sc[g - 1, rows, :] = lse
                else:
                    rows = pl.ds(pl.multiple_of(q0, 64), qb)
                    l1, l2 = lse_sc[0, rows, :], lse_sc[1, rows, :]
                    mx = jnp.maximum(jnp.maximum(lse, l1), l2)
                    e0, e1, e2 = jnp.exp(lse - mx), jnp.exp(l1 - mx), jnp.exp(l2 - mx)
                    y = (e0 * o + e1 * o_sc[0, rows, :] + e2 * o_sc[1, rows, :]) * (1.0 / (e0 + e1 + e2))
                    o_ref[0, rows, :] = (y * _silu(z_ref[0, rows, :].astype(F32))).astype(o_ref.dtype)
            return carry

        lax.fori_loop(0, (dil * nqb) // A_UNROLL, body, 0)


def attn_a(proj, proj_dil, slopes):
    b, l, _ = proj.shape
    ng = len(A_GROUPS)
    assert ng == 3 and (l // A_QB) % A_UNROLL == 0

    def spec(blk0):
        return pl.BlockSpec((1, l, HEAD_DIM), lambda i, h, blk0=blk0: (i, 0, blk0 + h))

    in_specs = [pl.BlockSpec(memory_space=pltpu.SMEM)]
    in_specs += [spec(s * A_HEADS) for s in range(3)]
    in_specs += [spec(c * A_HEADS) for c in range(3 * (ng - 1))]
    in_specs += [spec(PM_AZ // HEAD_DIM)]
    return pl.pallas_call(
        functools.partial(_attn_a_kernel, seq=l),
        grid=(b, A_HEADS),
        in_specs=in_specs,
        out_specs=pl.BlockSpec((1, l, HEAD_DIM), lambda i, h: (i, 0, h)),
        out_shape=jax.ShapeDtypeStruct((b, l, BRANCH_W), BF16),
        scratch_shapes=[pltpu.VMEM((3, A_QB, A_QB + 2 * A_RADIUS), F32),
                        pltpu.VMEM((ng - 1, l, LANES), F32), pltpu.VMEM((ng - 1, l, LANES), F32),
                        pltpu.VMEM((l, LANES), F32), pltpu.VMEM((l, LANES), F32)],
        compiler_params=_cparams(("parallel", "parallel")),
        name="attn_dilated",
    )(slopes, proj, proj, proj, *([proj_dil] * (3 * (ng - 1))), proj)


def _na_bias_tables(rpb, rows):
    kr = min(NA_KR, rows)
    qc = np.arange(GRID_W)
    cs = np.clip(qc - NA_KC // 2, 0, GRID_W - NA_KC)
    col_ok = (qc[None, :] >= cs[:, None]) & (qc[None, :] < cs[:, None] + NA_KC)
    dc = np.clip(qc[None, :] - qc[:, None] + NA_KC - 1, 0, 2 * NA_KC - 2)
    dc_onehot = (dc[..., None] == np.arange(2 * NA_KC - 1)).astype(np.float32)
    tabs = []
    for a in (0, 1, rows // NA_QROWS - 1):
        ks = int(np.clip(NA_QROWS * a - NA_QROWS, 0, rows - NA_KROWS))
        rq = NA_QROWS * a + np.arange(NA_QROWS)
        rs = np.clip(rq - kr // 2, 0, rows - kr)
        rk = ks + np.arange(NA_KROWS)
        row_ok = (rk[None, :] >= rs[:, None]) & (rk[None, :] < rs[:, None] + kr)
        dr = np.clip(rk[None, :] - rq[:, None] + NA_KR - 1, 0, 2 * NA_KR - 2)
        dr_onehot = (dr[..., None] == np.arange(2 * NA_KR - 1)).astype(np.float32)
        t = jnp.einsum("qjr,ckd,hrd->hqcjk", dr_onehot, dc_onehot, rpb.astype(F32),
                       precision=lax.Precision.HIGHEST)
        ok = row_ok[:, None, :, None] & col_ok[None, :, None, :]
        tabs.append(jnp.where(ok[None], t, NEG_INF).reshape(-1, NA_QT, NA_KT))
    return jnp.stack(tabs, axis=1)


NA_UNROLL = 2


def _attn_b_kernel(q_ref, k_ref, v_ref, z_ref, bias_ref, o_ref, *, seq):
    rows = seq // GRID_W
    nblk = rows // NA_QROWS
    scale = HEAD_DIM ** -0.5

    def body(it, carry):
        blks = [it * NA_UNROLL + u for u in range(NA_UNROLL)]
        ks_rows = [jnp.clip(NA_QROWS * a - NA_QROWS, 0, rows - NA_KROWS) for a in blks]
        qrows = [pl.ds(pl.multiple_of(a * NA_QT, NA_QT), NA_QT) for a in blks]
        krows = [pl.ds(pl.multiple_of(ks * GRID_W, GRID_W), NA_KT) for ks in ks_rows]
        scores = [lax.dot_general(q_ref[0, qr, :], k_ref[0, kr, :], (((1,), (1,)), ((), ())),
                                  preferred_element_type=F32) for qr, kr in zip(qrows, krows)]
        probs, sums = [], []
        for s, a, ks in zip(scores, blks, ks_rows):
            s = s * scale + bias_ref[0, (NA_QROWS * a - ks) // NA_QROWS]
            p = jnp.exp(s - jnp.max(s, axis=-1, keepdims=True))
            sums.append(jnp.sum(p, axis=-1, keepdims=True))
            probs.append(p.astype(BF16))
        outs = [jnp.dot(p, v_ref[0, kr, :], preferred_element_type=F32) for p, kr in zip(probs, krows)]
        for o, l, qr in zip(outs, sums, qrows):
            o_ref[0, qr, :] = (o * (1.0 / l) * _silu(z_ref[0, qr, :].astype(F32))).astype(o_ref.dtype)
        return carry

    lax.fori_loop(0, nblk // NA_UNROLL, body, 0)


def attn_b(proj, bias):
    b, l, _ = proj.shape
    assert l % (NA_QT * NA_UNROLL) == 0 and l // GRID_W >= NA_KROWS

    def col_spec(off):
        return pl.BlockSpec((1, l, HEAD_DIM), lambda h, i, off=off: (i, 0, off // HEAD_DIM + h))

    return pl.pallas_call(
        functools.partial(_attn_b_kernel, seq=l),
        grid=(B_HEADS, b),
        in_specs=[col_spec(PM_B), col_spec(PM_B + B_HEADS * HEAD_DIM),
                  col_spec(PM_B + 2 * B_HEADS * HEAD_DIM), col_spec(PM_BZ),
                  pl.BlockSpec((1, 3, NA_QT, NA_KT), lambda h, i: (h, 0, 0, 0))],
        out_specs=pl.BlockSpec((1, l, HEAD_DIM), lambda h, i: (i, 0, h)),
        out_shape=jax.ShapeDtypeStruct((b, l, BRANCH_W), BF16),
        compiler_params=_cparams(("parallel", "parallel")),
        name="attn_neighbourhood",
    )(proj, proj, proj, proj, bias)


HY_R = 4
HY_ND = 2 * HY_R - 1
HY_ROW0 = 16
HY_RC = 16


def _shift_down(x):
    t = lax.broadcasted_iota(jnp.int32, x.shape, 0)
    return jnp.where(t == 0, 0.0, pltpu.roll(x, 1, axis=0))


def _shift_up(x):
    n = x.shape[0]
    t = lax.broadcasted_iota(jnp.int32, x.shape, 0)
    return jnp.where(t == n - 1, 0.0, pltpu.roll(x, n - 1, axis=0))


def _conv3_kernel(u_ref, w_ref, b_ref, o_ref, *, sub):
    xs = [u_ref[0, pl.ds(p, sub, stride=HY_R), :] for p in range(HY_R)]
    w0, w1, w2, bias = w_ref[0:1, :], w_ref[1:2, :], w_ref[2:3, :], b_ref[...]
    for p in range(HY_R):
        prev = xs[p - 1] if p > 0 else _shift_down(xs[HY_R - 1])
        nxt = xs[p + 1] if p < HY_R - 1 else _shift_up(xs[0])
        o_ref[0, p] = (w0 * prev + w1 * xs[p] + w2 * nxt + bias).astype(o_ref.dtype)


def conv3(side, conv_w, conv_b):
    b, l, _ = side.shape
    sub = l // HY_R
    return pl.pallas_call(
        functools.partial(_conv3_kernel, sub=sub),
        grid=(b, C_PROJ // LANES),
        in_specs=[pl.BlockSpec((1, l, LANES), lambda i, j: (i, 0, SIDE_C // LANES + j)),
                  pl.BlockSpec((3, LANES), lambda i, j: (0, j)),
                  pl.BlockSpec((1, LANES), lambda i, j: (0, j))],
        out_specs=pl.BlockSpec((1, HY_R, sub, LANES), lambda i, j: (i, 0, 0, j)),
        out_shape=jax.ShapeDtypeStruct((b, HY_R, sub, C_PROJ), BF16),
        compiler_params=_cparams(("parallel", "parallel")),
        name="hyena_conv3",
    )(side, conv_w, conv_b.reshape(1, C_PROJ))


def _filter_kernel(feats_ref, w1_ref, b1_ref, w2_ref, b2_ref, freq_ref, w3_ref, dec_ref, o_ref, hid_ref,
                   *, tc):
    j = pl.program_id(0)
    hp = lax.Precision.HIGHEST

    @pl.when(j == 0)
    def _():
        h1 = jnp.sin(freq_ref[0:1, :] * (jnp.dot(feats_ref[...], w1_ref[...], precision=hp,
                                                 preferred_element_type=F32) + b1_ref[...]))
        hid_ref[...] = jnp.sin(freq_ref[1:2, :] * (jnp.dot(h1, w2_ref[...], precision=hp,
                                                           preferred_element_type=F32) + b2_ref[...]))

    raw = jnp.dot(hid_ref[...].astype(BF16), w3_ref[...].astype(BF16), preferred_element_type=F32)
    t_norm = feats_ref[:, 0:1]
    hfil = raw * jnp.exp(-t_norm * jnp.exp(dec_ref[...]))
    hfil = hfil * (1.0 / (jnp.sum(jnp.abs(hfil), axis=0, keepdims=True) + 1e-6))
    dropped_row = ((j * tc) // HY_CH) % 2 - 1
    t = lax.broadcasted_iota(jnp.int32, hfil.shape, 0)
    o_ref[...] = jnp.where(t == dropped_row, 0.0, hfil).astype(o_ref.dtype)


def hyena_filter_taps(seq, w1, b1, w2, b2, w3, freq, log_decay):
    t = jnp.concatenate([jnp.arange(p, seq, HY_R) for p in range(HY_R)]).astype(F32)
    t_norm = t / (seq - 1)
    ang = (2.0 * math.pi / seq) * t
    bands = jnp.linspace(1e-4, HY_BANDS - 1, HY_BANDS, dtype=F32)
    feats = jnp.concatenate([t_norm[:, None], jnp.cos(ang[:, None] * bands), -jnp.sin(ang[:, None] * bands)], axis=-1)
    emb = 64
    feats = jnp.pad(feats, ((0, 0), (0, emb - HY_EMB)))
    w1p = jnp.pad(w1, ((0, emb - HY_EMB), (0, 0)))
    ncol = HY_ORDER * 2 * HY_CH
    tc = 256
    full = lambda shape: pl.BlockSpec(shape, lambda j: (0, 0))
    return pl.pallas_call(
        functools.partial(_filter_kernel, tc=tc),
        grid=(ncol // tc,),
        in_specs=[full((seq, emb)), full((emb, HY_FFN)), full((1, HY_FFN)), full((HY_FFN, HY_FFN)),
                  full((1, HY_FFN)), full((2, HY_FFN)),
                  pl.BlockSpec((HY_FFN, tc), lambda j: (0, j)),
                  pl.BlockSpec((1, tc), lambda j: (0, j))],
        out_specs=pl.BlockSpec((seq, tc), lambda j: (0, j)),
        out_shape=jax.ShapeDtypeStruct((seq, ncol), BF16),
        scratch_shapes=[pltpu.VMEM((seq, HY_FFN), F32)],
        compiler_params=_cparams(("arbitrary",)),
        name="hyena_filter_taps",
    )(feats, w1p, b1.reshape(1, HY_FFN), w2, b2.reshape(1, HY_FFN), freq, w3, log_decay.reshape(1, ncol))


def _dft_consts(seq):
    sub = seq // HY_R
    k = jnp.arange(sub, dtype=jnp.int32)
    ang = ((k[:, None] * k[None, :]) % (2 * sub)).astype(F32) * (math.pi / sub)
    cmat = jnp.cos(ang)
    smat = jnp.sin(ang)
    sign = (1 - 2 * (k % 2)).astype(F32)
    smat = jnp.where((k == 0)[:, None], sign[None, :], smat)
    tw = k.astype(F32) * (math.pi / sub)
    cw = jnp.broadcast_to(jnp.cos(tw)[:, None], (sub, LANES))
    sw = jnp.broadcast_to(jnp.sin(tw)[:, None], (sub, LANES))
    return dict(cmat=cmat.astype(BF16), smat=smat.astype(BF16), smat_t=smat.T.astype(BF16), cw=cw, sw=sw)


def _residue_transforms(c_ref, s_ref, x_ref):
    cm, sm = c_ref[...], s_ref[...]
    dot = lambda a, b: jnp.dot(a, b, preferred_element_type=F32)
    return [(dot(cm, x_ref[p]), dot(sm, x_ref[p])) for p in range(HY_R)]


def _filter_spec_kernel(c_ref, s_ref, cw_ref, sw_ref, xf_ref, xb_ref, tab_ref, row0_ref, *, seq, tc):
    i = pl.program_id(2)
    fwd = _residue_transforms(c_ref, s_ref, xf_ref)
    bwd = _residue_transforms(c_ref, s_ref, xb_ref)
    c = jnp.tile(cw_ref[...], (1, tc // LANES))
    s = jnp.tile(sw_ref[...], (1, tc // LANES))
    inv_n = 1.0 / (2 * seq)
    sc = HY_R * 2.0 * inv_n
    mid = HY_R - 1

    def put(d, tr, ti):
        tab_ref[0, 2 * (d + mid)] = sc * tr
        tab_ref[0, 2 * (d + mid) + 1] = sc * ti

    put(0, fwd[0][0] + bwd[0][0], bwd[0][1] - fwd[0][1])
    for d in range(1, HY_R):
        (pf, qf), (pb, qb) = fwd[HY_R - d], bwd[d]
        put(d, c * pf - s * qf + pb, qb - c * qf - s * pf)
        (pf, qf), (pb, qb) = fwd[d], bwd[HY_R - d]
        put(-d, pf + c * pb - s * qb, c * qb + s * pb - qf)

    @pl.when(i == 0)
    def _():
        row0_ref[0] = jnp.zeros(row0_ref.shape[1:], F32)
        rn = HY_R * inv_n

        def put0(d, bin0, bin_m):
            row0_ref[0, d + mid:d + mid + 1, :] = rn * bin0
            row0_ref[0, 8 + d + mid:8 + d + mid + 1, :] = rn * bin_m

        put0(0, fwd[0][0][0:1] + bwd[0][0][0:1], fwd[0][1][0:1] + bwd[0][1][0:1])
        for d in range(1, HY_R):
            put0(d, fwd[HY_R - d][0][0:1] + bwd[d][0][0:1], bwd[d][1][0:1] - fwd[HY_R - d][1][0:1])
            put0(-d, fwd[d][0][0:1] + bwd[HY_R - d][0][0:1], fwd[d][1][0:1] - bwd[HY_R - d][1][0:1])


def hyena_filter_spectrum(taps, dc):
    seq = taps.shape[0]
    sub = seq // HY_R
    taps = taps.reshape(HY_R, sub, taps.shape[1])
    tf, tc = _tile(sub, 256), 256
    ncb = HY_CH // tc
    mspec = pl.BlockSpec((tf, sub), lambda o, j, i: (i, 0))
    tspec = pl.BlockSpec((tf, LANES), lambda o, j, i: (i, 0))
    return pl.pallas_call(
        functools.partial(_filter_spec_kernel, seq=seq, tc=tc),
        grid=(HY_ORDER, ncb, sub // tf),
        in_specs=[mspec, mspec, tspec, tspec,
                  pl.BlockSpec((HY_R, sub, tc), lambda o, j, i: (0, 0, (o * 2) * ncb + j)),
                  pl.BlockSpec((HY_R, sub, tc), lambda o, j, i: (0, 0, (o * 2 + 1) * ncb + j))],
        out_specs=[pl.BlockSpec((1, 2 * HY_ND, tf, tc), lambda o, j, i: (o, 0, i, j)),
                   pl.BlockSpec((1, HY_ROW0, tc), lambda o, j, i: (o, 0, j))],
        out_shape=[jax.ShapeDtypeStruct((HY_ORDER, 2 * HY_ND, sub, HY_CH), F32),
                   jax.ShapeDtypeStruct((HY_ORDER, HY_ROW0, HY_CH), F32)],
        compiler_params=_cparams(("parallel", "parallel", "arbitrary")),
        name="hyena_filter_spectrum",
    )(dc["cmat"], dc["smat"], dc["cw"], dc["sw"], taps, taps)


def _conv_fwd_kernel(c_ref, s_ref, z_ref, tab_ref, row0_ref, uv_ref, pq_ref, *, tf):
    i = pl.program_id(1)
    for p, (pp, qp) in enumerate(_residue_transforms(c_ref, s_ref, z_ref.at[0])):
        pq_ref[2 * p] = pp
        pq_ref[2 * p + 1] = qp
    mid = HY_R - 1

    def product(rows):
        for q in range(HY_R):
            u = v = None
            for p in range(HY_R):
                pp, qp = pq_ref[2 * p, rows, :], pq_ref[2 * p + 1, rows, :]
                tr, ti = tab_ref[0, 2 * (p - q + mid), rows, :], tab_ref[0, 2 * (p - q + mid) + 1, rows, :]
                du, dv = pp * tr + qp * ti, qp * tr - pp * ti
                u, v = (du, dv) if u is None else (u + du, v + dv)
            yield q, u, v

    def chunk(r, carry):
        rows = pl.ds(pl.multiple_of(r * HY_RC, HY_RC), HY_RC)
        for q, u, v in product(rows):
            uv_ref[0, 2 * q, rows, :] = u.astype(uv_ref.dtype)
            uv_ref[0, 2 * q + 1, rows, :] = v.astype(uv_ref.dtype)
        return carry

    lax.fori_loop(0, tf // HY_RC, chunk, 0)

    @pl.when(i == 0)
    def _():
        head = slice(0, HY_RC)
        first = lax.broadcasted_iota(jnp.int32, (HY_RC, 1), 0) == 0
        for q in range(HY_R):
            u0 = v0 = None
            for p in range(HY_R):
                du = pq_ref[2 * p, 0:1, :] * row0_ref[0, p - q + mid:p - q + mid + 1, :]
                dv = pq_ref[2 * p + 1, 0:1, :] * row0_ref[0, 8 + p - q + mid:8 + p - q + mid + 1, :]
                u0, v0 = (du, dv) if u0 is None else (u0 + du, v0 + dv)
            uv_ref[0, 2 * q, head, :] = jnp.where(first, u0, uv_ref[0, 2 * q, head, :].astype(F32)).astype(uv_ref.dtype)
            uv_ref[0, 2 * q + 1, head, :] = jnp.where(first, v0, uv_ref[0, 2 * q + 1, head, :].astype(F32)).astype(uv_ref.dtype)


def conv_fwd(z, zcol, dc, tabs, row0, order):
    b, _, sub, _ = z.shape
    tf, tc = _tile(sub, 256), 512
    ncb = HY_CH // tc
    mspec = pl.BlockSpec((tf, sub), lambda j, i, n: (i, 0))
    return pl.pallas_call(
        functools.partial(_conv_fwd_kernel, tf=tf),
        grid=(ncb, sub // tf, b),
        in_specs=[mspec, mspec,
                  pl.BlockSpec((1, HY_R, sub, tc), lambda j, i, n: (n, 0, 0, zcol // tc + j)),
                  pl.BlockSpec((1, 2 * HY_ND, tf, tc), lambda j, i, n: (order, 0, i, j)),
                  pl.BlockSpec((1, HY_ROW0, tc), lambda j, i, n: (order, 0, j))],
        out_specs=pl.BlockSpec((1, 2 * HY_R, tf, tc), lambda j, i, n: (n, 0, i, j)),
        out_shape=jax.ShapeDtypeStruct((b, 2 * HY_R, sub, HY_CH), BF16),
        scratch_shapes=[pltpu.VMEM((2 * HY_R, tf, tc), F32)],
        compiler_params=_cparams(("parallel", "parallel", "arbitrary")),
        name="hyena_conv_fwd",
    )(dc["cmat"], dc["smat"], z, tabs, row0)


def _conv_inv_kernel(c_ref, st_ref, uv_ref, x_ref, zp_ref, skip_ref, *rest, gated, tt, tc):
    cm, sm = c_ref[...], st_ref[...]
    dot = lambda a, b: jnp.dot(a, b, preferred_element_type=F32)
    skip = skip_ref[0]
    parts = []
    for q in range(HY_R):
        y = dot(cm, uv_ref[0, 2 * q]) + dot(sm, uv_ref[0, 2 * q + 1])
        parts.append(x_ref[0, q].astype(F32) * (y + zp_ref[0, q].astype(F32) * skip))
    if not gated:
        o_ref = rest[0]
        for q in range(HY_R):
            o_ref[0, q] = parts[q].astype(o_ref.dtype)
        return
    gate_ref, o_ref, il_ref = rest
    for s in range(tc // LANES):
        lanes = slice(s * LANES, (s + 1) * LANES)
        for q in range(HY_R):
            il_ref[s, pl.ds(q, tt, stride=HY_R), :] = parts[q][:, lanes]
    for s in range(tc // LANES):
        lanes = slice(s * LANES, (s + 1) * LANES)
        o_ref[0, :, lanes] = (il_ref[s] * _silu(gate_ref[0, :, lanes].astype(F32))).astype(o_ref.dtype)


def conv_inv(uv, dc, xmul, xcol, zprev, zcol, skip, gate=None, gcol=0):
    b, _, sub, _ = uv.shape
    tt, tc = _tile(sub, 256), 512
    ncb = HY_CH // tc
    mspec = pl.BlockSpec((tt, sub), lambda n, j, i: (i, 0))

    def tile_spec(col):
        return pl.BlockSpec((1, HY_R, tt, tc), lambda n, j, i, col=col: (n, 0, i, col // tc + j))

    in_specs = [mspec, mspec, pl.BlockSpec((1, 2 * HY_R, sub, tc), lambda n, j, i: (n, 0, 0, j)),
                tile_spec(xcol), tile_spec(zcol), pl.BlockSpec((1, 1, tc), lambda n, j, i: (0, 0, j))]
    args = [dc["cmat"], dc["smat_t"], uv, xmul, zprev, skip.reshape(1, 1, HY_CH)]
    scratch = []
    if gate is None:
        out_spec = pl.BlockSpec((1, HY_R, tt, tc), lambda n, j, i: (n, 0, i, j))
        out_shape = jax.ShapeDtypeStruct((b, HY_R, sub, HY_CH), BF16)
    else:
        in_specs.append(pl.BlockSpec((1, HY_R * tt, tc), lambda n, j, i: (n, i, gcol // tc + j)))
        args.append(gate)
        out_spec = pl.BlockSpec((1, HY_R * tt, tc), lambda n, j, i: (n, i, j))
        out_shape = jax.ShapeDtypeStruct((b, HY_R * sub, HY_CH), BF16)
        scratch = [pltpu.VMEM((tc // LANES, HY_R * tt, LANES), F32)]
    return pl.pallas_call(
        functools.partial(_conv_inv_kernel, gated=gate is not None, tt=tt, tc=tc),
        grid=(b, ncb, sub // tt),
        in_specs=in_specs,
        out_specs=out_spec,
        out_shape=out_shape,
        scratch_shapes=scratch,
        compiler_params=_cparams(("parallel", "parallel", "arbitrary")),
        name="hyena_conv_inv",
    )(*args)


def mixer_hyena(proj, side, dc, tabs, row0, conv_w, conv_b, skip):
    u = conv3(side, conv_w, conv_b)
    uv = conv_fwd(u, 0, dc, tabs, row0, 0)
    z1 = conv_inv(uv, dc, u, HY_CH, u, 0, skip[0])
    uv = conv_fwd(z1, 0, dc, tabs, row0, 1)
    return conv_inv(uv, dc, u, 2 * HY_CH, z1, 0, skip[1], gate=proj, gcol=PM_CZ)


def _merge_kernel(ya_ref, yb_ref, yc_ref, w_ref, g0_ref, g1_ref, g2_ref, o_ref):
    acc = None
    for y_ref, g_ref, n in ((ya_ref, g0_ref, 0), (yb_ref, g1_ref, 1), (yc_ref, g2_ref, 2)):
        t = _sigmoid(g_ref[...].astype(F32)) * jnp.dot(y_ref[...], w_ref[n], preferred_element_type=F32)
        acc = t if acc is None else acc + t
    o_ref[...] = acc.astype(o_ref.dtype)


def merge_branches(ya, yb, yc, w_branch, proj2d, d):
    m = ya.shape[0]
    tm, tn = _tile(m, 1024), _tile(d, 512)
    yspec = pl.BlockSpec((tm, BRANCH_W), lambda i, j: (i, 0))

    def gspec(n):
        return pl.BlockSpec((tm, tn), lambda i, j, n=n: (i, (PM_G + n * d) // tn + j))

    return pl.pallas_call(
        _merge_kernel,
        grid=(m // tm, d // tn),
        in_specs=[yspec, yspec, yspec,
                  pl.BlockSpec((N_BRANCH, BRANCH_W, tn), lambda i, j: (0, 0, j)),
                  gspec(0), gspec(1), gspec(2)],
        out_specs=pl.BlockSpec((tm, tn), lambda i, j: (i, j)),
        out_shape=jax.ShapeDtypeStruct((m, d), BF16),
        compiler_params=_cparams(("parallel", "arbitrary")),
        name="branch_merge",
    )(ya, yb, yc, w_branch, proj2d, proj2d, proj2d)


def _out_kernel(a_ref, w_ref, x_ref, mod_ref, o_ref):
    acc = jnp.dot(a_ref[0], w_ref[...], preferred_element_type=F32)
    o_ref[0] = x_ref[0] + mod_ref[0, 2:3, :] * acc


def out_proj(mixed, w_out, x, mod3):
    b, l, d = x.shape
    tm, tn = _tile(l, 1024), _tile(d, 512)
    return pl.pallas_call(
        _out_kernel,
        grid=(b, l // tm, d // tn),
        in_specs=[pl.BlockSpec((1, tm, d), lambda n, i, j: (n, i, 0)),
                  pl.BlockSpec((d, tn), lambda n, i, j: (0, j)),
                  pl.BlockSpec((1, tm, tn), lambda n, i, j: (n, i, j)),
                  pl.BlockSpec((1, 3, tn), lambda n, i, j: (n, 0, j))],
        out_specs=pl.BlockSpec((1, tm, tn), lambda n, i, j: (n, i, j)),
        out_shape=jax.ShapeDtypeStruct((b, l, d), F32),
        compiler_params=_cparams(("parallel", "parallel", "arbitrary")),
        name="out_proj",
    )(mixed, w_out, x, mod3)


def _trunk(x, mods, layers, w_in, dc, slopes, final_g):
    b, l, d = x.shape
    for layer, (lw, mod3) in enumerate(zip(layers, mods)):
        h = norm_mod(x, lw["norm_g"], mod3)
        proj2d, side2d = in_proj(h.reshape(b * l, d), w_in, layer, d)
        proj = proj2d.reshape(b, l, -1)
        side = side2d.reshape(b, l, -1)
        ya = attn_a(proj, side, slopes)
        yb = attn_b(proj, _na_bias_tables(lw["na_rpb"], l // GRID_W))
        taps = hyena_filter_taps(l, lw["hy_w1"], lw["hy_b1"], lw["hy_w2"], lw["hy_b2"], lw["hy_w3"],
                                 lw["hy_freq"], lw["hy_log_decay"])
        tabs, row0 = hyena_filter_spectrum(taps, dc)
        yc = mixer_hyena(proj, side, dc, tabs, row0, lw["conv_w"], lw["conv_b"], lw["hy_skip"])
        mixed = merge_branches(ya.reshape(b * l, -1), yb.reshape(b * l, -1), yc.reshape(b * l, -1),
                               lw["w_branch"], proj2d, d)
        x = out_proj(mixed.reshape(b, l, d), lw["w_out"], x, mod3)
    return final_norm(x, final_g)


def kernel(x_prompt, x_sample, c_prompt, c_sample, norm_g, w_ada, b_ada, w_in, w_branch, w_out, na_rpb, conv_w, conv_b, hy_w1, hy_b1, hy_w2, hy_b2, hy_w3, hy_freq, hy_log_decay, hy_skip, final_g):
    depth, d = norm_g.shape
    nbp = c_prompt.shape[0]
    c_all = jnp.concatenate([c_prompt, c_sample], axis=0)
    nslope = len(A_GROUPS) * A_HEADS
    slopes = jnp.asarray(2.0 ** (-8.0 * (np.arange(nslope) + 1) / nslope), F32)
    layers, mods_p, mods_s = [], [], []
    for i in range(depth):
        mod3 = ada_mod(c_all, w_ada, b_ada, i).reshape(c_all.shape[0], 3, d)
        mods_p.append(mod3[:nbp])
        mods_s.append(mod3[nbp:])
        layers.append(dict(
            norm_g=norm_g[i], w_branch=w_branch[i].astype(BF16),
            w_out=w_out[i].astype(BF16), na_rpb=na_rpb[i], conv_w=conv_w[i], conv_b=conv_b[i],
            hy_w1=hy_w1[i], hy_b1=hy_b1[i], hy_w2=hy_w2[i], hy_b2=hy_b2[i], hy_w3=hy_w3[i],
            hy_freq=hy_freq[i], hy_log_decay=hy_log_decay[i], hy_skip=hy_skip[i]))
    outs = []
    for x, mods in ((x_prompt, mods_p), (x_sample, mods_s)):
        outs.append(_trunk(x, mods, layers, w_in, _dft_consts(x.shape[1]), slopes, final_g))
    return tuple(outs)
```

```python
import functools
import math

import jax
import jax.numpy as jnp
import numpy as np
from jax import lax
from jax.experimental import pallas as pl
from jax.experimental.pallas import tpu as pltpu

F32 = jnp.float32
BF16 = jnp.bfloat16

LANES = 128
V7X_VMEM_LIMIT = 56 * 1024 * 1024

HEAD_DIM = 128
GRID_W = 64
BRANCH_W = 1024
N_BRANCH = 3
A_GROUPS = ((128, 1), (512, 4), (2048, 16))
A_HEADS = 8
A_RADIUS = 64
B_HEADS = 8
NA_KR = 8
NA_KC = 16
HY_ORDER = 2
HY_CH = BRANCH_W
HY_BANDS = 16
HY_EMB = 1 + 2 * HY_BANDS
HY_FFN = 64
RMS_EPS = 1e-6
NEG_INF = -1e30
LOG2E = math.log2(math.e)

A_QKV = 3 * len(A_GROUPS) * A_HEADS * HEAD_DIM
B_QKV = 3 * B_HEADS * HEAD_DIM
C_PROJ = (HY_ORDER + 1) * HY_CH
OFF_AZ = A_QKV
OFF_B = OFF_AZ + BRANCH_W
OFF_BZ = OFF_B + B_QKV
OFF_C = OFF_BZ + BRANCH_W
OFF_CZ = OFF_C + C_PROJ
OFF_G = OFF_CZ + BRANCH_W

PM_AZ = 3 * BRANCH_W
PM_B = PM_AZ + BRANCH_W
PM_BZ = PM_B + B_QKV
PM_CZ = PM_BZ + BRANCH_W
PM_G = PM_CZ + BRANCH_W
SIDE_C = 3 * (len(A_GROUPS) - 1) * BRANCH_W
SIDE_COLS = SIDE_C + C_PROJ

NA_QROWS = 4
NA_KROWS = 12
NA_QT = NA_QROWS * GRID_W
NA_KT = NA_KROWS * GRID_W

A_QB = 128


def _cparams(sem):
    return pltpu.CompilerParams(dimension_semantics=sem, vmem_limit_bytes=V7X_VMEM_LIMIT)


def _tile(n, pref):
    t = min(n, pref)
    while n % t:
        t //= 2
    return t


def _sigmoid(z):
    return 0.5 * jnp.tanh(0.5 * z) + 0.5


def _silu(z):
    return z * _sigmoid(z)


def _ada_kernel(c_ref, w_ref, b_ref, o_ref):
    a = _silu(c_ref[...]).astype(BF16)
    o_ref[...] = jnp.dot(a, w_ref[0].astype(BF16), preferred_element_type=F32) + b_ref[0]


def ada_mod(c, w_ada, b_ada, layer):
    nb, d = c.shape
    depth, _, n = w_ada.shape
    tn = _tile(n, 512)
    return pl.pallas_call(
        _ada_kernel,
        grid=(n // tn,),
        in_specs=[pl.BlockSpec((nb, d), lambda j: (0, 0)),
                  pl.BlockSpec((1, d, tn), lambda j: (layer, 0, j)),
                  pl.BlockSpec((1, 1, tn), lambda j: (layer, 0, j))],
        out_specs=pl.BlockSpec((nb, tn), lambda j: (0, j)),
        out_shape=jax.ShapeDtypeStruct((nb, n), F32),
        compiler_params=_cparams(("arbitrary",)),
        name="ada_mod",
    )(c, w_ada, b_ada.reshape(depth, 1, n))


def _norm_mod_kernel(x_ref, g_ref, mod_ref, o_ref):
    x = x_ref[0]
    y = x * lax.rsqrt(jnp.mean(x * x, axis=-1, keepdims=True) + RMS_EPS) * g_ref[...]
    shift = mod_ref[0, 0:1, :]
    scale = mod_ref[0, 1:2, :]
    o_ref[0] = (y * (1.0 + scale) + shift).astype(o_ref.dtype)


def norm_mod(x, g, mod3):
    b, l, d = x.shape
    tm = _tile(l, 256)
    return pl.pallas_call(
        _norm_mod_kernel,
        grid=(b, l // tm),
        in_specs=[pl.BlockSpec((1, tm, d), lambda i, t: (i, t, 0)),
                  pl.BlockSpec((1, d), lambda i, t: (0, 0)),
                  pl.BlockSpec((1, 3, d), lambda i, t: (i, 0, 0))],
        out_specs=pl.BlockSpec((1, tm, d), lambda i, t: (i, t, 0)),
        out_shape=jax.ShapeDtypeStruct((b, l, d), BF16),
        compiler_params=_cparams(("parallel", "parallel")),
        name="norm_mod",
    )(x, g.reshape(1, d), mod3)


def _final_norm_kernel(x_ref, g_ref, o_ref):
    x = x_ref[0]
    o_ref[0] = x * lax.rsqrt(jnp.mean(x * x, axis=-1, keepdims=True) + RMS_EPS) * g_ref[...]


def final_norm(x, g):
    b, l, d = x.shape
    tm = _tile(l, 256)
    return pl.pallas_call(
        _final_norm_kernel,
        grid=(b, l // tm),
        in_specs=[pl.BlockSpec((1, tm, d), lambda i, t: (i, t, 0)),
                  pl.BlockSpec((1, d), lambda i, t: (0, 0))],
        out_specs=pl.BlockSpec((1, tm, d), lambda i, t: (i, t, 0)),
        out_shape=jax.ShapeDtypeStruct((b, l, d), F32),
        compiler_params=_cparams(("parallel", "parallel")),
        name="final_norm",
    )(x, g.reshape(1, d))


def _mm_kernel(a_ref, w_ref, o_ref, wb_ref):
    @pl.when(pl.program_id(1) == 0)
    def _():
        wb_ref[...] = w_ref[0].astype(BF16)

    o_ref[...] = jnp.dot(a_ref[...], wb_ref[...], preferred_element_type=F32).astype(o_ref.dtype)


def matmul(a, w, layer, n_out, src_block, tn, out_dtype, name):
    m, k = a.shape
    tm = _tile(m, 1024)
    nrow = m // tm

    def row(j, i):
        return jnp.where(j % 2 == 0, i, nrow - 1 - i)

    return pl.pallas_call(
        _mm_kernel,
        grid=(n_out // tn, nrow),
        in_specs=[pl.BlockSpec((tm, k), lambda j, i: (row(j, i), 0)),
                  pl.BlockSpec((1, k, tn), lambda j, i: (layer, 0, src_block(j)))],
        out_specs=pl.BlockSpec((tm, tn), lambda j, i: (row(j, i), j)),
        out_shape=jax.ShapeDtypeStruct((m, n_out), out_dtype),
        scratch_shapes=[pltpu.VMEM((k, tn), BF16)],
        compiler_params=_cparams(("arbitrary", "arbitrary")),
        name=name,
    )(a, w)


def in_proj(h2d, w_in, layer, d):
    n_main = PM_G + N_BRANCH * d
    tn = _tile(n_main, 512)
    assert BRANCH_W % tn == 0
    per = BRANCH_W // tn
    ng = len(A_GROUPS)

    def main_src(j):
        return jnp.where(j < 3 * per, (j // per) * ng * per + j % per,
                         jnp.where(j < PM_CZ // tn, j + (OFF_AZ - PM_AZ) // tn, j + (OFF_CZ - PM_CZ) // tn))

    def side_src(j):
        c = j // per
        return jnp.where(j < SIDE_C // tn, ((c // 2) * ng + 1 + c % 2) * per + j % per, j + (OFF_C - SIDE_C) // tn)

    main = matmul(h2d, w_in, layer, n_main, main_src, tn, BF16, "in_proj")
    side = matmul(h2d, w_in, layer, SIDE_COLS, side_src, tn, F32, "in_proj_side")
    return main, side


A_UNROLL = 8
A_STAGE = 4


def _attn_a_kernel(slopes_ref, *refs, seq):
    qkv0 = refs[0:3]
    qkvd = refs[3:9]
    z_ref, o_ref = refs[9], refs[10]
    bias_sc, o_sc, lse_sc, k_st, v_st = refs[11:]
    h = pl.program_id(1)
    scale = HEAD_DIM ** -0.5 * LOG2E
    ng = len(A_GROUPS)

    for g in list(range(1, ng)) + [0]:
        dil = A_GROUPS[g][1]
        n = seq // dil
        qb = min(A_QB, n)
        slab = min(qb + 2 * A_RADIUS, n)
        nqb = n // qb
        if dil == 1:
            q_ref, k_ref, v_ref = (r.at[0] for r in qkv0)
        else:
            q_ref, k_ref, v_ref = (qkvd[s * 2 + g - 1].at[0] for s in range(3))
        staged = dil > A_STAGE
        if staged:
            quarter = seq // A_STAGE
            for src, dst in ((k_ref, k_st), (v_ref, v_st)):
                for r4 in range(A_STAGE):
                    dst[pl.ds(r4 * quarter, quarter), :] = src[pl.ds(r4, quarter, stride=A_STAGE), :]
            k_ref, v_ref = k_st, v_st
        kv_stride = dil // A_STAGE if staged else dil
        slope = slopes_ref[g * A_HEADS + h] * (float(dil) * LOG2E)

        col_minus_row = (lax.broadcasted_iota(jnp.int32, (qb, slab), 1)
                         - lax.broadcasted_iota(jnp.int32, (qb, slab), 0))
        for cls, off in enumerate((0, -A_RADIUS, qb - slab)):
            dist = jnp.abs(col_minus_row + off)
            bias_sc[cls, 0:qb, 0:slab] = jnp.where(dist <= A_RADIUS, -slope * dist.astype(F32), NEG_INF)

        def q_rows(q0, r, dil=dil, qb=qb, q_ref=q_ref):
            if dil == 1:
                return q_ref[pl.ds(pl.multiple_of(q0, 64), qb), :]
            return q_ref[pl.ds(q0 * dil + r, qb, stride=dil), :].astype(BF16)

        def kv_rows(ref, ks, r, dil=dil, slab=slab, staged=staged, kv_stride=kv_stride):
            if dil == 1:
                return ref[pl.ds(pl.multiple_of(ks, 64), slab), :]
            if staged:
                base = (r % A_STAGE) * (seq // A_STAGE) + r // A_STAGE
            else:
                base = r
            return ref[pl.ds(base + ks * kv_stride, slab, stride=kv_stride), :].astype(BF16)

        def body(it, carry, g=g, dil=dil, n=n, qb=qb, slab=slab, nqb=nqb, q_rows=q_rows, kv_rows=kv_rows,
                 k_ref=k_ref, v_ref=v_ref):
            blks = [it * A_UNROLL + u for u in range(A_UNROLL)]
            res = [blk // nqb for blk in blks]
            q0s = [(blk % nqb) * qb for blk in blks]
            kss = [jnp.clip(q0 - A_RADIUS, 0, n - slab) for q0 in q0s]
            scores = [lax.dot_general(q_rows(q0, r), kv_rows(k_ref, ks, r),
                                      (((1,), (1,)), ((), ())), preferred_element_type=F32)
                      for q0, ks, r in zip(q0s, kss, res)]
            probs, stats = [], []
            for s, q0 in zip(scores, q0s):
                cls = jnp.where(q0 == 0, 0, jnp.where(q0 + qb == n, 2, 1))
                s = s * scale + bias_sc[cls, 0:qb, 0:slab]
                m = jnp.max(s, axis=-1, keepdims=True)
                p = jnp.exp2(s - m)
                l = jnp.sum(p, axis=-1, keepdims=True)
                probs.append(p.astype(BF16))
                stats.append((m, l))
            outs = [jnp.dot(p, kv_rows(v_ref, ks, r), preferred_element_type=F32)
                    for p, ks, r in zip(probs, kss, res)]
            for o, (m, l), q0, r in zip(outs, stats, q0s, res):
                o = o * (1.0 / l)
                lse = jnp.broadcast_to(m + jnp.log(l) * LOG2E, (qb, LANES))
                if dil != 1:
                    rows = pl.ds(q0 * dil + r, qb, stride=dil)
                    o_sc[g - 1, rows, :] = o
                    lse_sc[g - 1, rows, :] = lse
                else:
                    rows = pl.ds(pl.multiple_of(q0, 64), qb)
                    l1, l2 = lse_sc[0, rows, :], lse_sc[1, rows, :]
                    mx = jnp.maximum(jnp.maximum(lse, l1), l2)
                    e0, e1, e2 = jnp.exp2(lse - mx), jnp.exp2(l1 - mx), jnp.exp2(l2 - mx)
                    y = (e0 * o + e1 * o_sc[0, rows, :] + e2 * o_sc[1, rows, :]) * (1.0 / (e0 + e1 + e2))
                    o_ref[0, rows, :] = (y * _silu(z_ref[0, rows, :].astype(F32))).astype(o_ref.dtype)
            return carry

        lax.fori_loop(0, (dil * nqb) // A_UNROLL, body, 0)


def attn_a(proj, proj_dil, slopes):
    b, l, _ = proj.shape
    ng = len(A_GROUPS)
    assert ng == 3 and (l // A_QB) % A_UNROLL == 0

    def spec(blk0):
        return pl.BlockSpec((1, l, HEAD_DIM), lambda i, h, blk0=blk0: (i, 0, blk0 + h))

    in_specs = [pl.BlockSpec(memory_space=pltpu.SMEM)]
    in_specs += [spec(s * A_HEADS) for s in range(3)]
    in_specs += [spec(c * A_HEADS) for c in range(3 * (ng - 1))]
    in_specs += [spec(PM_AZ // HEAD_DIM)]
    return pl.pallas_call(
        functools.partial(_attn_a_kernel, seq=l),
        grid=(b, A_HEADS),
        in_specs=in_specs,
        out_specs=pl.BlockSpec((1, l, HEAD_DIM), lambda i, h: (i, 0, h)),
        out_shape=jax.ShapeDtypeStruct((b, l, BRANCH_W), BF16),
        scratch_shapes=[pltpu.VMEM((3, A_QB, A_QB + 2 * A_RADIUS), F32),
                        pltpu.VMEM((ng - 1, l, LANES), F32), pltpu.VMEM((ng - 1, l, LANES), F32),
                        pltpu.VMEM((l, LANES), F32), pltpu.VMEM((l, LANES), F32)],
        compiler_params=_cparams(("parallel", "parallel")),
        name="attn_dilated",
    )(slopes, proj, proj, proj, *([proj_dil] * (3 * (ng - 1))), proj)


def _na_bias_tables(rpb, rows):
    kr = min(NA_KR, rows)
    qc = np.arange(GRID_W)
    cs = np.clip(qc - NA_KC // 2, 0, GRID_W - NA_KC)
    col_ok = (qc[None, :] >= cs[:, None]) & (qc[None, :] < cs[:, None] + NA_KC)
    dc = np.clip(qc[None, :] - qc[:, None] + NA_KC - 1, 0, 2 * NA_KC - 2)
    dc_onehot = (dc[..., None] == np.arange(2 * NA_KC - 1)).astype(np.float32)
    tabs = []
    for a in (0, 1, rows // NA_QROWS - 1):
        ks = int(np.clip(NA_QROWS * a - NA_QROWS, 0, rows - NA_KROWS))
        rq = NA_QROWS * a + np.arange(NA_QROWS)
        rs = np.clip(rq - kr // 2, 0, rows - kr)
        rk = ks + np.arange(NA_KROWS)
        row_ok = (rk[None, :] >= rs[:, None]) & (rk[None, :] < rs[:, None] + kr)
        dr = np.clip(rk[None, :] - rq[:, None] + NA_KR - 1, 0, 2 * NA_KR - 2)
        dr_onehot = (dr[..., None] == np.arange(2 * NA_KR - 1)).astype(np.float32)
        t = jnp.einsum("qjr,ckd,hrd->hqcjk", dr_onehot, dc_onehot, rpb.astype(F32),
                       precision=lax.Precision.HIGHEST)
        ok = row_ok[:, None, :, None] & col_ok[None, :, None, :]
        tabs.append(jnp.where(ok[None], t * LOG2E, NEG_INF).reshape(-1, NA_QT, NA_KT))
    return jnp.stack(tabs, axis=1)


NA_UNROLL = 2


def _attn_b_kernel(q_ref, k_ref, v_ref, z_ref, bias_ref, o_ref, *, seq):
    rows = seq // GRID_W
    nblk = rows // NA_QROWS
    scale = HEAD_DIM ** -0.5 * LOG2E

    def body(it, carry):
        blks = [it * NA_UNROLL + u for u in range(NA_UNROLL)]
        ks_rows = [jnp.clip(NA_QROWS * a - NA_QROWS, 0, rows - NA_KROWS) for a in blks]
        qrows = [pl.ds(pl.multiple_of(a * NA_QT, NA_QT), NA_QT) for a in blks]
        krows = [pl.ds(pl.multiple_of(ks * GRID_W, GRID_W), NA_KT) for ks in ks_rows]
        scores = [lax.dot_general(q_ref[0, qr, :], k_ref[0, kr, :], (((1,), (1,)), ((), ())),
                                  preferred_element_type=F32) for qr, kr in zip(qrows, krows)]
        probs, sums = [], []
        for s, a, ks in zip(scores, blks, ks_rows):
            s = s * scale + bias_ref[0, (NA_QROWS * a - ks) // NA_QROWS]
            p = jnp.exp2(s - jnp.max(s, axis=-1, keepdims=True))
            sums.append(jnp.sum(p, axis=-1, keepdims=True))
            probs.append(p.astype(BF16))
        outs = [jnp.dot(p, v_ref[0, kr, :], preferred_element_type=F32) for p, kr in zip(probs, krows)]
        for o, l, qr in zip(outs, sums, qrows):
            o_ref[0, qr, :] = (o * (1.0 / l) * _silu(z_ref[0, qr, :].astype(F32))).astype(o_ref.dtype)
        return carry

    lax.fori_loop(0, nblk // NA_UNROLL, body, 0)


def attn_b(proj, bias):
    b, l, _ = proj.shape
    assert l % (NA_QT * NA_UNROLL) == 0 and l // GRID_W >= NA_KROWS

    def col_spec(off):
        return pl.BlockSpec((1, l, HEAD_DIM), lambda h, i, off=off: (i, 0, off // HEAD_DIM + h))

    return pl.pallas_call(
        functools.partial(_attn_b_kernel, seq=l),
        grid=(B_HEADS, b),
        in_specs=[col_spec(PM_B), col_spec(PM_B + B_HEADS * HEAD_DIM),
                  col_spec(PM_B + 2 * B_HEADS * HEAD_DIM), col_spec(PM_BZ),
                  pl.BlockSpec((1, 3, NA_QT, NA_KT), lambda h, i: (h, 0, 0, 0))],
        out_specs=pl.BlockSpec((1, l, HEAD_DIM), lambda h, i: (i, 0, h)),
        out_shape=jax.ShapeDtypeStruct((b, l, BRANCH_W), BF16),
        compiler_params=_cparams(("parallel", "parallel")),
        name="attn_neighbourhood",
    )(proj, proj, proj, proj, bias)


HY_R = 4
HY_ND = 2 * HY_R - 1
HY_ROW0 = 16
HY_RC = 16


def _shift_down(x):
    t = lax.broadcasted_iota(jnp.int32, x.shape, 0)
    return jnp.where(t == 0, 0.0, pltpu.roll(x, 1, axis=0))


def _shift_up(x):
    n = x.shape[0]
    t = lax.broadcasted_iota(jnp.int32, x.shape, 0)
    return jnp.where(t == n - 1, 0.0, pltpu.roll(x, n - 1, axis=0))


def _conv3_kernel(u_ref, w_ref, b_ref, o_ref, *, sub):
    xs = [u_ref[0, pl.ds(p, sub, stride=HY_R), :] for p in range(HY_R)]
    w0, w1, w2, bias = w_ref[0:1, :], w_ref[1:2, :], w_ref[2:3, :], b_ref[...]
    for p in range(HY_R):
        prev = xs[p - 1] if p > 0 else _shift_down(xs[HY_R - 1])
        nxt = xs[p + 1] if p < HY_R - 1 else _shift_up(xs[0])
        o_ref[0, p] = (w0 * prev + w1 * xs[p] + w2 * nxt + bias).astype(o_ref.dtype)


def conv3(side, conv_w, conv_b):
    b, l, _ = side.shape
    sub = l // HY_R
    return pl.pallas_call(
        functools.partial(_conv3_kernel, sub=sub),
        grid=(b, C_PROJ // LANES),
        in_specs=[pl.BlockSpec((1, l, LANES), lambda i, j: (i, 0, SIDE_C // LANES + j)),
                  pl.BlockSpec((3, LANES), lambda i, j: (0, j)),
                  pl.BlockSpec((1, LANES), lambda i, j: (0, j))],
        out_specs=pl.BlockSpec((1, HY_R, sub, LANES), lambda i, j: (i, 0, 0, j)),
        out_shape=jax.ShapeDtypeStruct((b, HY_R, sub, C_PROJ), BF16),
        compiler_params=_cparams(("parallel", "parallel")),
        name="hyena_conv3",
    )(side, conv_w, conv_b.reshape(1, C_PROJ))


def _filter_kernel(feats_ref, w1_ref, b1_ref, w2_ref, b2_ref, freq_ref, w3_ref, dec_ref, o_ref, hid_ref,
                   *, tc):
    j = pl.program_id(0)
    hp = lax.Precision.HIGHEST

    @pl.when(j == 0)
    def _():
        h1 = jnp.sin(freq_ref[0:1, :] * (jnp.dot(feats_ref[...], w1_ref[...], precision=hp,
                                                 preferred_element_type=F32) + b1_ref[...]))
        hid_ref[...] = jnp.sin(freq_ref[1:2, :] * (jnp.dot(h1, w2_ref[...], precision=hp,
                                                           preferred_element_type=F32) + b2_ref[...]))

    raw = jnp.dot(hid_ref[...].astype(BF16), w3_ref[...].astype(BF16), preferred_element_type=F32)
    t_norm = feats_ref[:, 0:1]
    hfil = raw * jnp.exp(-t_norm * jnp.exp(dec_ref[...]))
    hfil = hfil * (1.0 / (jnp.sum(jnp.abs(hfil), axis=0, keepdims=True) + 1e-6))
    dropped_row = ((j * tc) // HY_CH) % 2 - 1
    t = lax.broadcasted_iota(jnp.int32, hfil.shape, 0)
    o_ref[...] = jnp.where(t == dropped_row, 0.0, hfil).astype(o_ref.dtype)


def hyena_filter_taps(seq, w1, b1, w2, b2, w3, freq, log_decay):
    t = jnp.concatenate([jnp.arange(p, seq, HY_R) for p in range(HY_R)]).astype(F32)
    t_norm = t / (seq - 1)
    ang = (2.0 * math.pi / seq) * t
    bands = jnp.linspace(1e-4, HY_BANDS - 1, HY_BANDS, dtype=F32)
    feats = jnp.concatenate([t_norm[:, None], jnp.cos(ang[:, None] * bands), -jnp.sin(ang[:, None] * bands)], axis=-1)
    emb = 64
    feats = jnp.pad(feats, ((0, 0), (0, emb - HY_EMB)))
    w1p = jnp.pad(w1, ((0, emb - HY_EMB), (0, 0)))
    ncol = HY_ORDER * 2 * HY_CH
    tc = 256
    full = lambda shape: pl.BlockSpec(shape, lambda j: (0, 0))
    return pl.pallas_call(
        functools.partial(_filter_kernel, tc=tc),
        grid=(ncol // tc,),
        in_specs=[full((seq, emb)), full((emb, HY_FFN)), full((1, HY_FFN)), full((HY_FFN, HY_FFN)),
                  full((1, HY_FFN)), full((2, HY_FFN)),
                  pl.BlockSpec((HY_FFN, tc), lambda j: (0, j)),
                  pl.BlockSpec((1, tc), lambda j: (0, j))],
        out_specs=pl.BlockSpec((seq, tc), lambda j: (0, j)),
        out_shape=jax.ShapeDtypeStruct((seq, ncol), BF16),
        scratch_shapes=[pltpu.VMEM((seq, HY_FFN), F32)],
        compiler_params=_cparams(("arbitrary",)),
        name="hyena_filter_taps",
    )(feats, w1p, b1.reshape(1, HY_FFN), w2, b2.reshape(1, HY_FFN), freq, w3, log_decay.reshape(1, ncol))


def _dft_consts(seq):
    sub = seq // HY_R
    k = jnp.arange(sub, dtype=jnp.int32)
    ang = ((k[:, None] * k[None, :]) % (2 * sub)).astype(F32) * (math.pi / sub)
    cmat = jnp.cos(ang)
    smat = jnp.sin(ang)
    sign = (1 - 2 * (k % 2)).astype(F32)
    smat = jnp.where((k == 0)[:, None], sign[None, :], smat)
    tw = k.astype(F32) * (math.pi / sub)
    cw = jnp.broadcast_to(jnp.cos(tw)[:, None], (sub, LANES))
    sw = jnp.broadcast_to(jnp.sin(tw)[:, None], (sub, LANES))
    return dict(cmat=cmat.astype(BF16), smat=smat.astype(BF16), smat_t=smat.T.astype(BF16), cw=cw, sw=sw)


def _residue_transforms(c_ref, s_ref, x_ref):
    cm, sm = c_ref[...], s_ref[...]
    dot = lambda a, b: jnp.dot(a, b, preferred_element_type=F32)
    return [(dot(cm, x_ref[p]), dot(sm, x_ref[p])) for p in range(HY_R)]


def _filter_spec_kernel(c_ref, s_ref, cw_ref, sw_ref, xf_ref, xb_ref, tab_ref, row0_ref, *, seq, tc):
    i = pl.program_id(2)
    fwd = _residue_transforms(c_ref, s_ref, xf_ref)
    bwd = _residue_transforms(c_ref, s_ref, xb_ref)
    c = jnp.tile(cw_ref[...], (1, tc // LANES))
    s = jnp.tile(sw_ref[...], (1, tc // LANES))
    inv_n = 1.0 / (2 * seq)
    sc = HY_R * 2.0 * inv_n
    mid = HY_R - 1

    def put(d, tr, ti):
        tab_ref[0, 2 * (d + mid)] = sc * tr
        tab_ref[0, 2 * (d + mid) + 1] = sc * ti

    put(0, fwd[0][0] + bwd[0][0], bwd[0][1] - fwd[0][1])
    for d in range(1, HY_R):
        (pf, qf), (pb, qb) = fwd[HY_R - d], bwd[d]
        put(d, c * pf - s * qf + pb, qb - c * qf - s * pf)
        (pf, qf), (pb, qb) = fwd[d], bwd[HY_R - d]
        put(-d, pf + c * pb - s * qb, c * qb + s * pb - qf)

    @pl.when(i == 0)
    def _():
        row0_ref[0] = jnp.zeros(row0_ref.shape[1:], F32)
        rn = HY_R * inv_n

        def put0(d, bin0, bin_m):
            row0_ref[0, d + mid:d + mid + 1, :] = rn * bin0
            row0_ref[0, 8 + d + mid:8 + d + mid + 1, :] = rn * bin_m

        put0(0, fwd[0][0][0:1] + bwd[0][0][0:1], fwd[0][1][0:1] + bwd[0][1][0:1])
        for d in range(1, HY_R):
            put0(d, fwd[HY_R - d][0][0:1] + bwd[d][0][0:1], bwd[d][1][0:1] - fwd[HY_R - d][1][0:1])
            put0(-d, fwd[d][0][0:1] + bwd[HY_R - d][0][0:1], fwd[d][1][0:1] - bwd[HY_R - d][1][0:1])


def hyena_filter_spectrum(taps, dc):
    seq = taps.shape[0]
    sub = seq // HY_R
    taps = taps.reshape(HY_R, sub, taps.shape[1])
    tf, tc = _tile(sub, 256), 256
    ncb = HY_CH // tc
    mspec = pl.BlockSpec((tf, sub), lambda o, j, i: (i, 0))
    tspec = pl.BlockSpec((tf, LANES), lambda o, j, i: (i, 0))
    return pl.pallas_call(
        functools.partial(_filter_spec_kernel, seq=seq, tc=tc),
        grid=(HY_ORDER, ncb, sub // tf),
        in_specs=[mspec, mspec, tspec, tspec,
                  pl.BlockSpec((HY_R, sub, tc), lambda o, j, i: (0, 0, (o * 2) * ncb + j)),
                  pl.BlockSpec((HY_R, sub, tc), lambda o, j, i: (0, 0, (o * 2 + 1) * ncb + j))],
        out_specs=[pl.BlockSpec((1, 2 * HY_ND, tf, tc), lambda o, j, i: (o, 0, i, j)),
                   pl.BlockSpec((1, HY_ROW0, tc), lambda o, j, i: (o, 0, j))],
        out_shape=[jax.ShapeDtypeStruct((HY_ORDER, 2 * HY_ND, sub, HY_CH), F32),
                   jax.ShapeDtypeStruct((HY_ORDER, HY_ROW0, HY_CH), F32)],
        compiler_params=_cparams(("parallel", "parallel", "arbitrary")),
        name="hyena_filter_spectrum",
    )(dc["cmat"], dc["smat"], dc["cw"], dc["sw"], taps, taps)


def _conv_fwd_kernel(c_ref, s_ref, z_ref, tab_ref, row0_ref, uv_ref, pq_ref, *, tf, tc):
    i = pl.program_id(1)
    cm, sm = c_ref[...], s_ref[...]
    dot = lambda a, b: jnp.dot(a, b, preferred_element_type=F32)
    mid = HY_R - 1
    halves = [slice(0, tc // 2), slice(tc // 2, tc)]

    def transforms(lanes):
        for p in range(HY_R):
            x = z_ref[0, p, :, lanes]
            pq_ref[2 * p, :, lanes] = dot(cm, x)
            pq_ref[2 * p + 1, :, lanes] = dot(sm, x)

    def products(lanes):
        for r in range(tf // HY_RC):
            rows = slice(r * HY_RC, (r + 1) * HY_RC)
            for q in range(HY_R):
                u = v = None
                for p in range(HY_R):
                    pp, qp = pq_ref[2 * p, rows, lanes], pq_ref[2 * p + 1, rows, lanes]
                    tr = tab_ref[0, 2 * (p - q + mid), rows, lanes]
                    ti = tab_ref[0, 2 * (p - q + mid) + 1, rows, lanes]
                    du, dv = pp * tr + qp * ti, qp * tr - pp * ti
                    u, v = (du, dv) if u is None else (u + du, v + dv)
                uv_ref[0, 2 * q, rows, lanes] = u.astype(uv_ref.dtype)
                uv_ref[0, 2 * q + 1, rows, lanes] = v.astype(uv_ref.dtype)

    transforms(halves[0])
    transforms(halves[1])
    products(halves[0])
    products(halves[1])

    @pl.when(i == 0)
    def _():
        head = slice(0, HY_RC)
        first = lax.broadcasted_iota(jnp.int32, (HY_RC, 1), 0) == 0
        for q in range(HY_R):
            u0 = v0 = None
            for p in range(HY_R):
                du = pq_ref[2 * p, 0:1, :] * row0_ref[0, p - q + mid:p - q + mid + 1, :]
                dv = pq_ref[2 * p + 1, 0:1, :] * row0_ref[0, 8 + p - q + mid:8 + p - q + mid + 1, :]
                u0, v0 = (du, dv) if u0 is None else (u0 + du, v0 + dv)
            uv_ref[0, 2 * q, head, :] = jnp.where(first, u0, uv_ref[0, 2 * q, head, :].astype(F32)).astype(uv_ref.dtype)
            uv_ref[0, 2 * q + 1, head, :] = jnp.where(first, v0, uv_ref[0, 2 * q + 1, head, :].astype(F32)).astype(uv_ref.dtype)


def conv_fwd(z, zcol, dc, tabs, row0, order):
    b, _, sub, _ = z.shape
    tf, tc = _tile(sub, 256), 512
    ncb = HY_CH // tc
    mspec = pl.BlockSpec((tf, sub), lambda j, i, n: (i, 0))
    return pl.pallas_call(
        functools.partial(_conv_fwd_kernel, tf=tf, tc=tc),
        grid=(ncb, sub // tf, b),
        in_specs=[mspec, mspec,
                  pl.BlockSpec((1, HY_R, sub, tc), lambda j, i, n: (n, 0, 0, zcol // tc + j)),
                  pl.BlockSpec((1, 2 * HY_ND, tf, tc), lambda j, i, n: (order, 0, i, j)),
                  pl.BlockSpec((1, HY_ROW0, tc), lambda j, i, n: (order, 0, j))],
        out_specs=pl.BlockSpec((1, 2 * HY_R, tf, tc), lambda j, i, n: (n, 0, i, j)),
        out_shape=jax.ShapeDtypeStruct((b, 2 * HY_R, sub, HY_CH), BF16),
        scratch_shapes=[pltpu.VMEM((2 * HY_R, tf, tc), F32)],
        compiler_params=_cparams(("parallel", "parallel", "arbitrary")),
        name="hyena_conv_fwd",
    )(dc["cmat"], dc["smat"], z, tabs, row0)


def _conv_inv_kernel(c_ref, st_ref, uv_ref, x_ref, zp_ref, skip_ref, *rest, gated, tt, tc):
    cm, sm = c_ref[...], st_ref[...]
    dot = lambda a, b: jnp.dot(a, b, preferred_element_type=F32)
    skip = skip_ref[0]
    parts = []
    for q in range(HY_R):
        y = dot(cm, uv_ref[0, 2 * q]) + dot(sm, uv_ref[0, 2 * q + 1])
        parts.append(x_ref[0, q].astype(F32) * (y + zp_ref[0, q].astype(F32) * skip))
    if not gated:
        o_ref = rest[0]
        for q in range(HY_R):
            o_ref[0, q] = parts[q].astype(o_ref.dtype)
        return
    gate_ref, o_ref, il_ref = rest
    for s in range(tc // LANES):
        lanes = slice(s * LANES, (s + 1) * LANES)
        for q in range(HY_R):
            il_ref[s, pl.ds(q, tt, stride=HY_R), :] = parts[q][:, lanes]
    for s in range(tc // LANES):
        lanes = slice(s * LANES, (s + 1) * LANES)
        o_ref[0, :, lanes] = (il_ref[s] * _silu(gate_ref[0, :, lanes].astype(F32))).astype(o_ref.dtype)


def conv_inv(uv, dc, xmul, xcol, zprev, zcol, skip, gate=None, gcol=0):
    b, _, sub, _ = uv.shape
    tt, tc = _tile(sub, 256), 512
    ncb = HY_CH // tc
    mspec = pl.BlockSpec((tt, sub), lambda n, j, i: (i, 0))

    def tile_spec(col):
        return pl.BlockSpec((1, HY_R, tt, tc), lambda n, j, i, col=col: (n, 0, i, col // tc + j))

    in_specs = [mspec, mspec, pl.BlockSpec((1, 2 * HY_R, sub, tc), lambda n, j, i: (n, 0, 0, j)),
                tile_spec(xcol), tile_spec(zcol), pl.BlockSpec((1, 1, tc), lambda n, j, i: (0, 0, j))]
    args = [dc["cmat"], dc["smat_t"], uv, xmul, zprev, skip.reshape(1, 1, HY_CH)]
    scratch = []
    if gate is None:
        out_spec = pl.BlockSpec((1, HY_R, tt, tc), lambda n, j, i: (n, 0, i, j))
        out_shape = jax.ShapeDtypeStruct((b, HY_R, sub, HY_CH), BF16)
    else:
        in_specs.append(pl.BlockSpec((1, HY_R * tt, tc), lambda n, j, i: (n, i, gcol // tc + j)))
        args.append(gate)
        out_spec = pl.BlockSpec((1, HY_R * tt, tc), lambda n, j, i: (n, i, j))
        out_shape = jax.ShapeDtypeStruct((b, HY_R * sub, HY_CH), BF16)
        scratch = [pltpu.VMEM((tc // LANES, HY_R * tt, LANES), F32)]
    return pl.pallas_call(
        functools.partial(_conv_inv_kernel, gated=gate is not None, tt=tt, tc=tc),
        grid=(b, ncb, sub // tt),
        in_specs=in_specs,
        out_specs=out_spec,
        out_shape=out_shape,
        scratch_shapes=scratch,
        compiler_params=_cparams(("parallel", "parallel", "arbitrary")),
        name="hyena_conv_inv",
    )(*args)


def mixer_hyena(proj, side, dc, tabs, row0, conv_w, conv_b, skip):
    u = conv3(side, conv_w, conv_b)
    uv = conv_fwd(u, 0, dc, tabs, row0, 0)
    z1 = conv_inv(uv, dc, u, HY_CH, u, 0, skip[0])
    uv = conv_fwd(z1, 0, dc, tabs, row0, 1)
    return conv_inv(uv, dc, u, 2 * HY_CH, z1, 0, skip[1], gate=proj, gcol=PM_CZ)


def _merge_kernel(ya_ref, yb_ref, yc_ref, w_ref, g0_ref, g1_ref, g2_ref, o_ref):
    acc = None
    for y_ref, g_ref, n in ((ya_ref, g0_ref, 0), (yb_ref, g1_ref, 1), (yc_ref, g2_ref, 2)):
        t = _sigmoid(g_ref[...].astype(F32)) * jnp.dot(y_ref[...], w_ref[n], preferred_element_type=F32)
        acc = t if acc is None else acc + t
    o_ref[...] = acc.astype(o_ref.dtype)


def merge_branches(ya, yb, yc, w_branch, proj2d, d):
    m = ya.shape[0]
    tm, tn = _tile(m, 1024), _tile(d, 512)
    yspec = pl.BlockSpec((tm, BRANCH_W), lambda i, j: (i, 0))

    def gspec(n):
        return pl.BlockSpec((tm, tn), lambda i, j, n=n: (i, (PM_G + n * d) // tn + j))

    return pl.pallas_call(
        _merge_kernel,
        grid=(m // tm, d // tn),
        in_specs=[yspec, yspec, yspec,
                  pl.BlockSpec((N_BRANCH, BRANCH_W, tn), lambda i, j: (0, 0, j)),
                  gspec(0), gspec(1), gspec(2)],
        out_specs=pl.BlockSpec((tm, tn), lambda i, j: (i, j)),
        out_shape=jax.ShapeDtypeStruct((m, d), BF16),
        compiler_params=_cparams(("parallel", "arbitrary")),
        name="branch_merge",
    )(ya, yb, yc, w_branch, proj2d, proj2d, proj2d)


def _out_kernel(a_ref, w_ref, x_ref, mod_ref, o_ref):
    acc = jnp.dot(a_ref[0], w_ref[...], preferred_element_type=F32)
    o_ref[0] = x_ref[0] + mod_ref[0, 2:3, :] * acc


def out_proj(mixed, w_out, x, mod3):
    b, l, d = x.shape
    tm, tn = _tile(l, 1024), _tile(d, 512)
    return pl.pallas_call(
        _out_kernel,
        grid=(b, l // tm, d // tn),
        in_specs=[pl.BlockSpec((1, tm, d), lambda n, i, j: (n, i, 0)),
                  pl.BlockSpec((d, tn), lambda n, i, j: (0, j)),
                  pl.BlockSpec((1, tm, tn), lambda n, i, j: (n, i, j)),
                  pl.BlockSpec((1, 3, tn), lambda n, i, j: (n, 0, j))],
        out_specs=pl.BlockSpec((1, tm, tn), lambda n, i, j: (n, i, j)),
        out_shape=jax.ShapeDtypeStruct((b, l, d), F32),
        compiler_params=_cparams(("parallel", "parallel", "arbitrary")),
        name="out_proj",
    )(mixed, w_out, x, mod3)


def _trunk(x, mods, layers, w_in, dc, slopes, final_g):
    b, l, d = x.shape
    for layer, (lw, mod3) in enumerate(zip(layers, mods)):
        h = norm_mod(x, lw["norm_g"], mod3)
        proj2d, side2d = in_proj(h.reshape(b * l, d), w_in, layer, d)
        proj = proj2d.reshape(b, l, -1)
        side = side2d.reshape(b, l, -1)
        ya = attn_a(proj, side, slopes)
        yb = attn_b(proj, _na_bias_tables(lw["na_rpb"], l // GRID_W))
        taps = hyena_filter_taps(l, lw["hy_w1"], lw["hy_b1"], lw["hy_w2"], lw["hy_b2"], lw["hy_w3"],
                                 lw["hy_freq"], lw["hy_log_decay"])
        tabs, row0 = hyena_filter_spectrum(taps, dc)
        yc = mixer_hyena(proj, side, dc, tabs, row0, lw["conv_w"], lw["conv_b"], lw["hy_skip"])
        mixed = merge_branches(ya.reshape(b * l, -1), yb.reshape(b * l, -1), yc.reshape(b * l, -1),
                               lw["w_branch"], proj2d, d)
        x = out_proj(mixed.reshape(b, l, d), lw["w_out"], x, mod3)
    return final_norm(x, final_g)


def kernel(x_prompt, x_sample, c_prompt, c_sample, norm_g, w_ada, b_ada, w_in, w_branch, w_out, na_rpb, conv_w, conv_b, hy_w1, hy_b1, hy_w2, hy_b2, hy_w3, hy_freq, hy_log_decay, hy_skip, final_g):
    depth, d = norm_g.shape
    nbp = c_prompt.shape[0]
    c_all = jnp.concatenate([c_prompt, c_sample], axis=0)
    nslope = len(A_GROUPS) * A_HEADS
    slopes = jnp.asarray(2.0 ** (-8.0 * (np.arange(nslope) + 1) / nslope), F32)
    layers, mods_p, mods_s = [], [], []
    for i in range(depth):
        mod3 = ada_mod(c_all, w_ada, b_ada, i).reshape(c_all.shape[0], 3, d)
        mods_p.append(mod3[:nbp])
        mods_s.append(mod3[nbp:])
        layers.append(dict(
            norm_g=norm_g[i], w_branch=w_branch[i].astype(BF16),
            w_out=w_out[i].astype(BF16), na_rpb=na_rpb[i], conv_w=conv_w[i], conv_b=conv_b[i],
            hy_w1=hy_w1[i], hy_b1=hy_b1[i], hy_w2=hy_w2[i], hy_b2=hy_b2[i], hy_w3=hy_w3[i],
            hy_freq=hy_freq[i], hy_log_decay=hy_log_decay[i], hy_skip=hy_skip[i]))
    outs = []
    for x, mods in ((x_prompt, mods_p), (x_sample, mods_s)):
        outs.append(_trunk(x, mods, layers, w_in, _dft_consts(x.shape[1]), slopes, final_g))
    return tuple(outs)
```

```python
import functools
import math

import jax
import jax.numpy as jnp
import numpy as np
from jax import lax
from jax.experimental import pallas as pl
from jax.experimental.pallas import tpu as pltpu

F32 = jnp.float32
BF16 = jnp.bfloat16

LANES = 128
V7X_VMEM_LIMIT = 56 * 1024 * 1024

HEAD_DIM = 128
GRID_W = 64
BRANCH_W = 1024
N_BRANCH = 3
A_GROUPS = ((128, 1), (512, 4), (2048, 16))
A_HEADS = 8
A_RADIUS = 64
B_HEADS = 8
NA_KR = 8
NA_KC = 16
HY_ORDER = 2
HY_CH = BRANCH_W
HY_BANDS = 16
HY_EMB = 1 + 2 * HY_BANDS
HY_FFN = 64
RMS_EPS = 1e-6
NEG_INF = -1e30
LOG2E = math.log2(math.e)

A_QKV = 3 * len(A_GROUPS) * A_HEADS * HEAD_DIM
B_QKV = 3 * B_HEADS * HEAD_DIM
C_PROJ = (HY_ORDER + 1) * HY_CH
OFF_AZ = A_QKV
OFF_B = OFF_AZ + BRANCH_W
OFF_BZ = OFF_B + B_QKV
OFF_C = OFF_BZ + BRANCH_W
OFF_CZ = OFF_C + C_PROJ
OFF_G = OFF_CZ + BRANCH_W

PM_AZ = 3 * BRANCH_W
PM_B = PM_AZ + BRANCH_W
PM_BZ = PM_B + B_QKV
PM_CZ = PM_BZ + BRANCH_W
PM_G = PM_CZ + BRANCH_W
SIDE_C = 3 * (len(A_GROUPS) - 1) * BRANCH_W
SIDE_COLS = SIDE_C + C_PROJ

NA_QROWS = 4
NA_KROWS = 12
NA_QT = NA_QROWS * GRID_W
NA_KT = NA_KROWS * GRID_W

A_QB = 128


def _cparams(sem):
    return pltpu.CompilerParams(dimension_semantics=sem, vmem_limit_bytes=V7X_VMEM_LIMIT)


def _tile(n, pref):
    t = min(n, pref)
    while n % t:
        t //= 2
    return t


def _sigmoid(z):
    return 0.5 * jnp.tanh(0.5 * z) + 0.5


def _silu(z):
    return z * _sigmoid(z)


def _ada_kernel(c_ref, w_ref, b_ref, o_ref):
    a = _silu(c_ref[...]).astype(BF16)
    o_ref[...] = jnp.dot(a, w_ref[0].astype(BF16), preferred_element_type=F32) + b_ref[0]


def ada_mod(c, w_ada, b_ada, layer):
    nb, d = c.shape
    depth, _, n = w_ada.shape
    tn = _tile(n, 512)
    return pl.pallas_call(
        _ada_kernel,
        grid=(n // tn,),
        in_specs=[pl.BlockSpec((nb, d), lambda j: (0, 0)),
                  pl.BlockSpec((1, d, tn), lambda j: (layer, 0, j)),
                  pl.BlockSpec((1, 1, tn), lambda j: (layer, 0, j))],
        out_specs=pl.BlockSpec((nb, tn), lambda j: (0, j)),
        out_shape=jax.ShapeDtypeStruct((nb, n), F32),
        compiler_params=_cparams(("arbitrary",)),
        name="ada_mod",
    )(c, w_ada, b_ada.reshape(depth, 1, n))


def _norm_mod_kernel(x_ref, g_ref, mod_ref, o_ref):
    x = x_ref[0]
    y = x * lax.rsqrt(jnp.mean(x * x, axis=-1, keepdims=True) + RMS_EPS) * g_ref[...]
    shift = mod_ref[0, 0:1, :]
    scale = mod_ref[0, 1:2, :]
    o_ref[0] = (y * (1.0 + scale) + shift).astype(o_ref.dtype)


def norm_mod(x, g, mod3):
    b, l, d = x.shape
    tm = _tile(l, 256)
    return pl.pallas_call(
        _norm_mod_kernel,
        grid=(b, l // tm),
        in_specs=[pl.BlockSpec((1, tm, d), lambda i, t: (i, t, 0)),
                  pl.BlockSpec((1, d), lambda i, t: (0, 0)),
                  pl.BlockSpec((1, 3, d), lambda i, t: (i, 0, 0))],
        out_specs=pl.BlockSpec((1, tm, d), lambda i, t: (i, t, 0)),
        out_shape=jax.ShapeDtypeStruct((b, l, d), BF16),
        compiler_params=_cparams(("parallel", "parallel")),
        name="norm_mod",
    )(x, g.reshape(1, d), mod3)


def _final_norm_kernel(x_ref, g_ref, o_ref):
    x = x_ref[0]
    o_ref[0] = x * lax.rsqrt(jnp.mean(x * x, axis=-1, keepdims=True) + RMS_EPS) * g_ref[...]


def final_norm(x, g):
    b, l, d = x.shape
    tm = _tile(l, 256)
    return pl.pallas_call(
        _final_norm_kernel,
        grid=(b, l // tm),
        in_specs=[pl.BlockSpec((1, tm, d), lambda i, t: (i, t, 0)),
                  pl.BlockSpec((1, d), lambda i, t: (0, 0))],
        out_specs=pl.BlockSpec((1, tm, d), lambda i, t: (i, t, 0)),
        out_shape=jax.ShapeDtypeStruct((b, l, d), F32),
        compiler_params=_cparams(("parallel", "parallel")),
        name="final_norm",
    )(x, g.reshape(1, d))


def _mm_kernel(a_ref, w_ref, o_ref, wb_ref):
    @pl.when(pl.program_id(1) == 0)
    def _():
        wb_ref[...] = w_ref[0].astype(BF16)

    o_ref[...] = jnp.dot(a_ref[...], wb_ref[...], preferred_element_type=F32).astype(o_ref.dtype)


def matmul(a, w, layer, n_out, src_block, tn, out_dtype, name):
    m, k = a.shape
    tm = _tile(m, 1024)
    nrow = m // tm

    def row(j, i):
        return jnp.where(j % 2 == 0, i, nrow - 1 - i)

    return pl.pallas_call(
        _mm_kernel,
        grid=(n_out // tn, nrow),
        in_specs=[pl.BlockSpec((tm, k), lambda j, i: (row(j, i), 0)),
                  pl.BlockSpec((1, k, tn), lambda j, i: (layer, 0, src_block(j)))],
        out_specs=pl.BlockSpec((tm, tn), lambda j, i: (row(j, i), j)),
        out_shape=jax.ShapeDtypeStruct((m, n_out), out_dtype),
        scratch_shapes=[pltpu.VMEM((k, tn), BF16)],
        compiler_params=_cparams(("arbitrary", "arbitrary")),
        name=name,
    )(a, w)


def in_proj(h2d, w_in, layer, d):
    n_main = PM_G + N_BRANCH * d
    tn = _tile(n_main, 512)
    assert BRANCH_W % tn == 0
    per = BRANCH_W // tn
    ng = len(A_GROUPS)

    def main_src(j):
        return jnp.where(j < 3 * per, (j // per) * ng * per + j % per,
                         jnp.where(j < PM_CZ // tn, j + (OFF_AZ - PM_AZ) // tn, j + (OFF_CZ - PM_CZ) // tn))

    def side_src(j):
        c = j // per
        return jnp.where(j < SIDE_C // tn, ((c // 2) * ng + 1 + c % 2) * per + j % per, j + (OFF_C - SIDE_C) // tn)

    main = matmul(h2d, w_in, layer, n_main, main_src, tn, BF16, "in_proj")
    side = matmul(h2d, w_in, layer, SIDE_COLS, side_src, tn, F32, "in_proj_side")
    return main, side


A_UNROLL = 8
A_STAGE = 4


def _attn_a_kernel(slopes_ref, *refs, seq):
    qkv0 = refs[0:3]
    qkvd = refs[3:9]
    z_ref, o_ref = refs[9], refs[10]
    bias_sc, o_sc, lse_sc, k_st, v_st = refs[11:]
    h = pl.program_id(1)
    scale = HEAD_DIM ** -0.5 * LOG2E
    ng = len(A_GROUPS)

    for g in list(range(1, ng)) + [0]:
        dil = A_GROUPS[g][1]
        n = seq // dil
        qb = min(A_QB, n)
        slab = min(qb + 2 * A_RADIUS, n)
        nqb = n // qb
        if dil == 1:
            q_ref, k_ref, v_ref = (r.at[0] for r in qkv0)
        else:
            q_ref, k_ref, v_ref = (qkvd[s * 2 + g - 1].at[0] for s in range(3))
        staged = dil > A_STAGE
        if staged:
            quarter = seq // A_STAGE
            for src, dst in ((k_ref, k_st), (v_ref, v_st)):
                for r4 in range(A_STAGE):
                    dst[pl.ds(r4 * quarter, quarter), :] = src[pl.ds(r4, quarter, stride=A_STAGE), :]
            k_ref, v_ref = k_st, v_st
        kv_stride = dil // A_STAGE if staged else dil
        slope = slopes_ref[g * A_HEADS + h] * (float(dil) * LOG2E)

        col_minus_row = (lax.broadcasted_iota(jnp.int32, (qb, slab), 1)
                         - lax.broadcasted_iota(jnp.int32, (qb, slab), 0))
        for cls, off in enumerate((0, -A_RADIUS, qb - slab)):
            dist = jnp.abs(col_minus_row + off)
            bias_sc[cls, 0:qb, 0:slab] = jnp.where(dist <= A_RADIUS, -slope * dist.astype(F32), NEG_INF)

        def q_rows(q0, r, dil=dil, qb=qb, q_ref=q_ref):
            if dil == 1:
                return q_ref[pl.ds(pl.multiple_of(q0, 64), qb), :]
            return q_ref[pl.ds(q0 * dil + r, qb, stride=dil), :].astype(BF16)

        def kv_rows(ref, ks, r, dil=dil, slab=slab, staged=staged, kv_stride=kv_stride):
            if dil == 1:
                return ref[pl.ds(pl.multiple_of(ks, 64), slab), :]
            if staged:
                base = (r % A_STAGE) * (seq // A_STAGE) + r // A_STAGE
            else:
                base = r
            return ref[pl.ds(base + ks * kv_stride, slab, stride=kv_stride), :].astype(BF16)

        def body(it, carry, g=g, dil=dil, n=n, qb=qb, slab=slab, nqb=nqb, q_rows=q_rows, kv_rows=kv_rows,
                 k_ref=k_ref, v_ref=v_ref):
            blks = [it * A_UNROLL + u for u in range(A_UNROLL)]
            res = [blk // nqb for blk in blks]
            q0s = [(blk % nqb) * qb for blk in blks]
            kss = [jnp.clip(q0 - A_RADIUS, 0, n - slab) for q0 in q0s]
            scores = [lax.dot_general(q_rows(q0, r), kv_rows(k_ref, ks, r),
                                      (((1,), (1,)), ((), ())), preferred_element_type=F32)
                      for q0, ks, r in zip(q0s, kss, res)]
            probs, stats = [], []
            for s, q0 in zip(scores, q0s):
                cls = jnp.where(q0 == 0, 0, jnp.where(q0 + qb == n, 2, 1))
                s = s * scale + bias_sc[cls, 0:qb, 0:slab]
                m = jnp.max(s, axis=-1, keepdims=True)
                p = jnp.exp2(s - m)
                l = jnp.sum(p, axis=-1, keepdims=True)
                probs.append(p.astype(BF16))
                stats.append((m, l))
            outs = [jnp.dot(p, kv_rows(v_ref, ks, r), preferred_element_type=F32)
                    for p, ks, r in zip(probs, kss, res)]
            for o, (m, l), q0, r in zip(outs, stats, q0s, res):
                o = o * (1.0 / l)
                lse = jnp.broadcast_to(m + jnp.log(l) * LOG2E, (qb, LANES))
                if dil != 1:
                    rows = pl.ds(q0 * dil + r, qb, stride=dil)
                    o_sc[g - 1, rows, :] = o
                    lse_sc[g - 1, rows, :] = lse
                else:
                    rows = pl.ds(pl.multiple_of(q0, 64), qb)
                    l1, l2 = lse_sc[0, rows, :], lse_sc[1, rows, :]
                    mx = jnp.maximum(jnp.maximum(lse, l1), l2)
                    e0, e1, e2 = jnp.exp2(lse - mx), jnp.exp2(l1 - mx), jnp.exp2(l2 - mx)
                    y = (e0 * o + e1 * o_sc[0, rows, :] + e2 * o_sc[1, rows, :]) * (1.0 / (e0 + e1 + e2))
                    o_ref[0, rows, :] = (y * _silu(z_ref[0, rows, :].astype(F32))).astype(o_ref.dtype)
            return carry

        lax.fori_loop(0, (dil * nqb) // A_UNROLL, body, 0)


def attn_a(proj, proj_dil, slopes):
    b, l, _ = proj.shape
    ng = len(A_GROUPS)
    assert ng == 3 and (l // A_QB) % A_UNROLL == 0

    def spec(blk0):
        return pl.BlockSpec((1, l, HEAD_DIM), lambda i, h, blk0=blk0: (i, 0, blk0 + h))

    in_specs = [pl.BlockSpec(memory_space=pltpu.SMEM)]
    in_specs += [spec(s * A_HEADS) for s in range(3)]
    in_specs += [spec(c * A_HEADS) for c in range(3 * (ng - 1))]
    in_specs += [spec(PM_AZ // HEAD_DIM)]
    return pl.pallas_call(
        functools.partial(_attn_a_kernel, seq=l),
        grid=(b, A_HEADS),
        in_specs=in_specs,
        out_specs=pl.BlockSpec((1, l, HEAD_DIM), lambda i, h: (i, 0, h)),
        out_shape=jax.ShapeDtypeStruct((b, l, BRANCH_W), BF16),
        scratch_shapes=[pltpu.VMEM((3, A_QB, A_QB + 2 * A_RADIUS), F32),
                        pltpu.VMEM((ng - 1, l, LANES), F32), pltpu.VMEM((ng - 1, l, LANES), F32),
                        pltpu.VMEM((l, LANES), F32), pltpu.VMEM((l, LANES), F32)],
        compiler_params=_cparams(("parallel", "parallel")),
        name="attn_dilated",
    )(slopes, proj, proj, proj, *([proj_dil] * (3 * (ng - 1))), proj)


def _na_bias_tables(rpb, rows):
    kr = min(NA_KR, rows)
    qc = np.arange(GRID_W)
    cs = np.clip(qc - NA_KC // 2, 0, GRID_W - NA_KC)
    col_ok = (qc[None, :] >= cs[:, None]) & (qc[None, :] < cs[:, None] + NA_KC)
    dc = np.clip(qc[None, :] - qc[:, None] + NA_KC - 1, 0, 2 * NA_KC - 2)
    dc_onehot = (dc[..., None] == np.arange(2 * NA_KC - 1)).astype(np.float32)
    tabs = []
    for a in (0, 1, rows // NA_QROWS - 1):
        ks = int(np.clip(NA_QROWS * a - NA_QROWS, 0, rows - NA_KROWS))
        rq = NA_QROWS * a + np.arange(NA_QROWS)
        rs = np.clip(rq - kr // 2, 0, rows - kr)
        rk = ks + np.arange(NA_KROWS)
        row_ok = (rk[None, :] >= rs[:, None]) & (rk[None, :] < rs[:, None] + kr)
        dr = np.clip(rk[None, :] - rq[:, None] + NA_KR - 1, 0, 2 * NA_KR - 2)
        dr_onehot = (dr[..., None] == np.arange(2 * NA_KR - 1)).astype(np.float32)
        t = jnp.einsum("qjr,ckd,hrd->hqcjk", dr_onehot, dc_onehot, rpb.astype(F32),
                       precision=lax.Precision.HIGHEST)
        ok = row_ok[:, None, :, None] & col_ok[None, :, None, :]
        tabs.append(jnp.where(ok[None], t * LOG2E, NEG_INF).reshape(-1, NA_QT, NA_KT))
    return jnp.stack(tabs, axis=1)


NA_UNROLL = 4


def _attn_b_kernel(q_ref, k_ref, v_ref, z_ref, bias_ref, o_ref, *, seq):
    rows = seq // GRID_W
    nblk = rows // NA_QROWS
    scale = HEAD_DIM ** -0.5 * LOG2E

    def body(it, carry):
        blks = [it * NA_UNROLL + u for u in range(NA_UNROLL)]
        ks_rows = [jnp.clip(NA_QROWS * a - NA_QROWS, 0, rows - NA_KROWS) for a in blks]
        qrows = [pl.ds(pl.multiple_of(a * NA_QT, NA_QT), NA_QT) for a in blks]
        krows = [pl.ds(pl.multiple_of(ks * GRID_W, GRID_W), NA_KT) for ks in ks_rows]
        scores = [lax.dot_general(q_ref[0, qr, :], k_ref[0, kr, :], (((1,), (1,)), ((), ())),
                                  preferred_element_type=F32) for qr, kr in zip(qrows, krows)]
        probs, sums = [], []
        for s, a, ks in zip(scores, blks, ks_rows):
            s = s * scale + bias_ref[0, (NA_QROWS * a - ks) // NA_QROWS]
            p = jnp.exp2(s - jnp.max(s, axis=-1, keepdims=True))
            sums.append(jnp.sum(p, axis=-1, keepdims=True))
            probs.append(p.astype(BF16))
        outs = [jnp.dot(p, v_ref[0, kr, :], preferred_element_type=F32) for p, kr in zip(probs, krows)]
        for o, l, qr in zip(outs, sums, qrows):
            o_ref[0, qr, :] = (o * (1.0 / l) * _silu(z_ref[0, qr, :].astype(F32))).astype(o_ref.dtype)
        return carry

    lax.fori_loop(0, nblk // NA_UNROLL, body, 0)


def attn_b(proj, bias):
    b, l, _ = proj.shape
    assert l % (NA_QT * NA_UNROLL) == 0 and l // GRID_W >= NA_KROWS

    def col_spec(off):
        return pl.BlockSpec((1, l, HEAD_DIM), lambda h, i, off=off: (i, 0, off // HEAD_DIM + h))

    return pl.pallas_call(
        functools.partial(_attn_b_kernel, seq=l),
        grid=(B_HEADS, b),
        in_specs=[col_spec(PM_B), col_spec(PM_B + B_HEADS * HEAD_DIM),
                  col_spec(PM_B + 2 * B_HEADS * HEAD_DIM), col_spec(PM_BZ),
                  pl.BlockSpec((1, 3, NA_QT, NA_KT), lambda h, i: (h, 0, 0, 0))],
        out_specs=pl.BlockSpec((1, l, HEAD_DIM), lambda h, i: (i, 0, h)),
        out_shape=jax.ShapeDtypeStruct((b, l, BRANCH_W), BF16),
        compiler_params=_cparams(("parallel", "parallel")),
        name="attn_neighbourhood",
    )(proj, proj, proj, proj, bias)


HY_R = 4
HY_ND = 2 * HY_R - 1
HY_ROW0 = 16
HY_RC = 16


def _shift_down(x):
    t = lax.broadcasted_iota(jnp.int32, x.shape, 0)
    return jnp.where(t == 0, 0.0, pltpu.roll(x, 1, axis=0))


def _shift_up(x):
    n = x.shape[0]
    t = lax.broadcasted_iota(jnp.int32, x.shape, 0)
    return jnp.where(t == n - 1, 0.0, pltpu.roll(x, n - 1, axis=0))


def _conv3_kernel(u_ref, w_ref, b_ref, o_ref, *, sub):
    xs = [u_ref[0, pl.ds(p, sub, stride=HY_R), :] for p in range(HY_R)]
    w0, w1, w2, bias = w_ref[0:1, :], w_ref[1:2, :], w_ref[2:3, :], b_ref[...]
    for p in range(HY_R):
        prev = xs[p - 1] if p > 0 else _shift_down(xs[HY_R - 1])
        nxt = xs[p + 1] if p < HY_R - 1 else _shift_up(xs[0])
        o_ref[0, p] = (w0 * prev + w1 * xs[p] + w2 * nxt + bias).astype(o_ref.dtype)


def conv3(side, conv_w, conv_b):
    b, l, _ = side.shape
    sub = l // HY_R
    return pl.pallas_call(
        functools.partial(_conv3_kernel, sub=sub),
        grid=(b, C_PROJ // LANES),
        in_specs=[pl.BlockSpec((1, l, LANES), lambda i, j: (i, 0, SIDE_C // LANES + j)),
                  pl.BlockSpec((3, LANES), lambda i, j: (0, j)),
                  pl.BlockSpec((1, LANES), lambda i, j: (0, j))],
        out_specs=pl.BlockSpec((1, HY_R, sub, LANES), lambda i, j: (i, 0, 0, j)),
        out_shape=jax.ShapeDtypeStruct((b, HY_R, sub, C_PROJ), BF16),
        compiler_params=_cparams(("parallel", "parallel")),
        name="hyena_conv3",
    )(side, conv_w, conv_b.reshape(1, C_PROJ))


def _filter_kernel(feats_ref, w1_ref, b1_ref, w2_ref, b2_ref, freq_ref, w3_ref, dec_ref, o_ref, hid_ref,
                   *, tc):
    j = pl.program_id(0)
    hp = lax.Precision.HIGHEST

    @pl.when(j == 0)
    def _():
        h1 = jnp.sin(freq_ref[0:1, :] * (jnp.dot(feats_ref[...], w1_ref[...], precision=hp,
                                                 preferred_element_type=F32) + b1_ref[...]))
        hid_ref[...] = jnp.sin(freq_ref[1:2, :] * (jnp.dot(h1, w2_ref[...], precision=hp,
                                                           preferred_element_type=F32) + b2_ref[...]))

    raw = jnp.dot(hid_ref[...].astype(BF16), w3_ref[...].astype(BF16), preferred_element_type=F32)
    t_norm = feats_ref[:, 0:1]
    hfil = raw * jnp.exp(-t_norm * jnp.exp(dec_ref[...]))
    hfil = hfil * (1.0 / (jnp.sum(jnp.abs(hfil), axis=0, keepdims=True) + 1e-6))
    dropped_row = ((j * tc) // HY_CH) % 2 - 1
    t = lax.broadcasted_iota(jnp.int32, hfil.shape, 0)
    o_ref[...] = jnp.where(t == dropped_row, 0.0, hfil).astype(o_ref.dtype)


def hyena_filter_taps(seq, w1, b1, w2, b2, w3, freq, log_decay):
    t = jnp.concatenate([jnp.arange(p, seq, HY_R) for p in range(HY_R)]).astype(F32)
    t_norm = t / (seq - 1)
    ang = (2.0 * math.pi / seq) * t
    bands = jnp.linspace(1e-4, HY_BANDS - 1, HY_BANDS, dtype=F32)
    feats = jnp.concatenate([t_norm[:, None], jnp.cos(ang[:, None] * bands), -jnp.sin(ang[:, None] * bands)], axis=-1)
    emb = 64
    feats = jnp.pad(feats, ((0, 0), (0, emb - HY_EMB)))
    w1p = jnp.pad(w1, ((0, emb - HY_EMB), (0, 0)))
    ncol = HY_ORDER * 2 * HY_CH
    tc = 256
    full = lambda shape: pl.BlockSpec(shape, lambda j: (0, 0))
    return pl.pallas_call(
        functools.partial(_filter_kernel, tc=tc),
        grid=(ncol // tc,),
        in_specs=[full((seq, emb)), full((emb, HY_FFN)), full((1, HY_FFN)), full((HY_FFN, HY_FFN)),
                  full((1, HY_FFN)), full((2, HY_FFN)),
                  pl.BlockSpec((HY_FFN, tc), lambda j: (0, j)),
                  pl.BlockSpec((1, tc), lambda j: (0, j))],
        out_specs=pl.BlockSpec((seq, tc), lambda j: (0, j)),
        out_shape=jax.ShapeDtypeStruct((seq, ncol), BF16),
        scratch_shapes=[pltpu.VMEM((seq, HY_FFN), F32)],
        compiler_params=_cparams(("arbitrary",)),
        name="hyena_filter_taps",
    )(feats, w1p, b1.reshape(1, HY_FFN), w2, b2.reshape(1, HY_FFN), freq, w3, log_decay.reshape(1, ncol))


def _dft_consts(seq):
    sub = seq // HY_R
    k = jnp.arange(sub, dtype=jnp.int32)
    ang = ((k[:, None] * k[None, :]) % (2 * sub)).astype(F32) * (math.pi / sub)
    cmat = jnp.cos(ang)
    smat = jnp.sin(ang)
    sign = (1 - 2 * (k % 2)).astype(F32)
    smat = jnp.where((k == 0)[:, None], sign[None, :], smat)
    tw = k.astype(F32) * (math.pi / sub)
    cw = jnp.broadcast_to(jnp.cos(tw)[:, None], (sub, LANES))
    sw = jnp.broadcast_to(jnp.sin(tw)[:, None], (sub, LANES))
    return dict(cmat=cmat.astype(BF16), smat=smat.astype(BF16), smat_t=smat.T.astype(BF16), cw=cw, sw=sw)


def _residue_transforms(c_ref, s_ref, x_ref):
    cm, sm = c_ref[...], s_ref[...]
    dot = lambda a, b: jnp.dot(a, b, preferred_element_type=F32)
    return [(dot(cm, x_ref[p]), dot(sm, x_ref[p])) for p in range(HY_R)]


def _filter_spec_kernel(c_ref, s_ref, cw_ref, sw_ref, xf_ref, xb_ref, tab_ref, row0_ref, *, seq, tc):
    i = pl.program_id(2)
    fwd = _residue_transforms(c_ref, s_ref, xf_ref)
    bwd = _residue_transforms(c_ref, s_ref, xb_ref)
    c = jnp.tile(cw_ref[...], (1, tc // LANES))
    s = jnp.tile(sw_ref[...], (1, tc // LANES))
    inv_n = 1.0 / (2 * seq)
    sc = HY_R * 2.0 * inv_n
    mid = HY_R - 1

    def put(d, tr, ti):
        tab_ref[0, 2 * (d + mid)] = sc * tr
        tab_ref[0, 2 * (d + mid) + 1] = sc * ti

    put(0, fwd[0][0] + bwd[0][0], bwd[0][1] - fwd[0][1])
    for d in range(1, HY_R):
        (pf, qf), (pb, qb) = fwd[HY_R - d], bwd[d]
        put(d, c * pf - s * qf + pb, qb - c * qf - s * pf)
        (pf, qf), (pb, qb) = fwd[d], bwd[HY_R - d]
        put(-d, pf + c * pb - s * qb, c * qb + s * pb - qf)

    @pl.when(i == 0)
    def _():
        row0_ref[0] = jnp.zeros(row0_ref.shape[1:], F32)
        rn = HY_R * inv_n

        def put0(d, bin0, bin_m):
            row0_ref[0, d + mid:d + mid + 1, :] = rn * bin0
            row0_ref[0, 8 + d + mid:8 + d + mid + 1, :] = rn * bin_m

        put0(0, fwd[0][0][0:1] + bwd[0][0][0:1], fwd[0][1][0:1] + bwd[0][1][0:1])
        for d in range(1, HY_R):
            put0(d, fwd[HY_R - d][0][0:1] + bwd[d][0][0:1], bwd[d][1][0:1] - fwd[HY_R - d][1][0:1])
            put0(-d, fwd[d][0][0:1] + bwd[HY_R - d][0][0:1], fwd[d][1][0:1] - bwd[HY_R - d][1][0:1])


def hyena_filter_spectrum(taps, dc):
    seq = taps.shape[0]
    sub = seq // HY_R
    taps = taps.reshape(HY_R, sub, taps.shape[1])
    tf, tc = _tile(sub, 256), 256
    ncb = HY_CH // tc
    mspec = pl.BlockSpec((tf, sub), lambda o, j, i: (i, 0))
    tspec = pl.BlockSpec((tf, LANES), lambda o, j, i: (i, 0))
    return pl.pallas_call(
        functools.partial(_filter_spec_kernel, seq=seq, tc=tc),
        grid=(HY_ORDER, ncb, sub // tf),
        in_specs=[mspec, mspec, tspec, tspec,
                  pl.BlockSpec((HY_R, sub, tc), lambda o, j, i: (0, 0, (o * 2) * ncb + j)),
                  pl.BlockSpec((HY_R, sub, tc), lambda o, j, i: (0, 0, (o * 2 + 1) * ncb + j))],
        out_specs=[pl.BlockSpec((1, 2 * HY_ND, tf, tc), lambda o, j, i: (o, 0, i, j)),
                   pl.BlockSpec((1, HY_ROW0, tc), lambda o, j, i: (o, 0, j))],
        out_shape=[jax.ShapeDtypeStruct((HY_ORDER, 2 * HY_ND, sub, HY_CH), F32),
                   jax.ShapeDtypeStruct((HY_ORDER, HY_ROW0, HY_CH), F32)],
        compiler_params=_cparams(("parallel", "parallel", "arbitrary")),
        name="hyena_filter_spectrum",
    )(dc["cmat"], dc["smat"], dc["cw"], dc["sw"], taps, taps)


def _conv_fwd_kernel(c_ref, s_ref, z_ref, tab_ref, row0_ref, uv_ref, pq_ref, *, tf, tc):
    i = pl.program_id(1)
    cm, sm = c_ref[...], s_ref[...]
    dot = lambda a, b: jnp.dot(a, b, preferred_element_type=F32)
    mid = HY_R - 1
    halves = [slice(0, tc // 2), slice(tc // 2, tc)]

    def transforms(lanes):
        for p in range(HY_R):
            x = z_ref[0, p, :, lanes]
            pq_ref[2 * p, :, lanes] = dot(cm, x)
            pq_ref[2 * p + 1, :, lanes] = dot(sm, x)

    def products(lanes):
        for r in range(tf // HY_RC):
            rows = slice(r * HY_RC, (r + 1) * HY_RC)
            for q in range(HY_R):
                u = v = None
                for p in range(HY_R):
                    pp, qp = pq_ref[2 * p, rows, lanes], pq_ref[2 * p + 1, rows, lanes]
                    tr = tab_ref[0, 2 * (p - q + mid), rows, lanes]
                    ti = tab_ref[0, 2 * (p - q + mid) + 1, rows, lanes]
                    du, dv = pp * tr + qp * ti, qp * tr - pp * ti
                    u, v = (du, dv) if u is None else (u + du, v + dv)
                uv_ref[0, 2 * q, rows, lanes] = u.astype(uv_ref.dtype)
                uv_ref[0, 2 * q + 1, rows, lanes] = v.astype(uv_ref.dtype)

    transforms(halves[0])
    transforms(halves[1])
    products(halves[0])
    products(halves[1])

    @pl.when(i == 0)
    def _():
        head = slice(0, HY_RC)
        first = lax.broadcasted_iota(jnp.int32, (HY_RC, 1), 0) == 0
        for q in range(HY_R):
            u0 = v0 = None
            for p in range(HY_R):
                du = pq_ref[2 * p, 0:1, :] * row0_ref[0, p - q + mid:p - q + mid + 1, :]
                dv = pq_ref[2 * p + 1, 0:1, :] * row0_ref[0, 8 + p - q + mid:8 + p - q + mid + 1, :]
                u0, v0 = (du, dv) if u0 is None else (u0 + du, v0 + dv)
            uv_ref[0, 2 * q, head, :] = jnp.where(first, u0, uv_ref[0, 2 * q, head, :].astype(F32)).astype(uv_ref.dtype)
            uv_ref[0, 2 * q + 1, head, :] = jnp.where(first, v0, uv_ref[0, 2 * q + 1, head, :].astype(F32)).astype(uv_ref.dtype)


def conv_fwd(z, zcol, dc, tabs, row0, order):
    b, _, sub, _ = z.shape
    tf, tc = _tile(sub, 256), 512
    ncb = HY_CH // tc
    mspec = pl.BlockSpec((tf, sub), lambda j, i, n: (i, 0))
    return pl.pallas_call(
        functools.partial(_conv_fwd_kernel, tf=tf, tc=tc),
        grid=(ncb, sub // tf, b),
        in_specs=[mspec, mspec,
                  pl.BlockSpec((1, HY_R, sub, tc), lambda j, i, n: (n, 0, 0, zcol // tc + j)),
                  pl.BlockSpec((1, 2 * HY_ND, tf, tc), lambda j, i, n: (order, 0, i, j)),
                  pl.BlockSpec((1, HY_ROW0, tc), lambda j, i, n: (order, 0, j))],
        out_specs=pl.BlockSpec((1, 2 * HY_R, tf, tc), lambda j, i, n: (n, 0, i, j)),
        out_shape=jax.ShapeDtypeStruct((b, 2 * HY_R, sub, HY_CH), BF16),
        scratch_shapes=[pltpu.VMEM((2 * HY_R, tf, tc), F32)],
        compiler_params=_cparams(("parallel", "parallel", "arbitrary")),
        name="hyena_conv_fwd",
    )(dc["cmat"], dc["smat"], z, tabs, row0)


def _conv_inv_kernel(c_ref, st_ref, uv_ref, x_ref, zp_ref, skip_ref, *rest, gated, tt, tc):
    cm, sm = c_ref[...], st_ref[...]
    dot = lambda a, b: jnp.dot(a, b, preferred_element_type=F32)
    skip = skip_ref[0]
    parts = []
    for q in range(HY_R):
        y = dot(cm, uv_ref[0, 2 * q]) + dot(sm, uv_ref[0, 2 * q + 1])
        parts.append(x_ref[0, q].astype(F32) * (y + zp_ref[0, q].astype(F32) * skip))
    if not gated:
        o_ref = rest[0]
        for q in range(HY_R):
            o_ref[0, q] = parts[q].astype(o_ref.dtype)
        return
    gate_ref, o_ref, il_ref = rest
    for s in range(tc // LANES):
        lanes = slice(s * LANES, (s + 1) * LANES)
        for q in range(HY_R):
            il_ref[s, pl.ds(q, tt, stride=HY_R), :] = parts[q][:, lanes]
    for s in range(tc // LANES):
        lanes = slice(s * LANES, (s + 1) * LANES)
        o_ref[0, :, lanes] = (il_ref[s] * _silu(gate_ref[0, :, lanes].astype(F32))).astype(o_ref.dtype)


def conv_inv(uv, dc, xmul, xcol, zprev, zcol, skip, gate=None, gcol=0):
    b, _, sub, _ = uv.shape
    tt, tc = _tile(sub, 256), 512
    ncb = HY_CH // tc
    mspec = pl.BlockSpec((tt, sub), lambda n, j, i: (i, 0))

    def tile_spec(col):
        return pl.BlockSpec((1, HY_R, tt, tc), lambda n, j, i, col=col: (n, 0, i, col // tc + j))

    in_specs = [mspec, mspec, pl.BlockSpec((1, 2 * HY_R, sub, tc), lambda n, j, i: (n, 0, 0, j)),
                tile_spec(xcol), tile_spec(zcol), pl.BlockSpec((1, 1, tc), lambda n, j, i: (0, 0, j))]
    args = [dc["cmat"], dc["smat_t"], uv, xmul, zprev, skip.reshape(1, 1, HY_CH)]
    scratch = []
    if gate is None:
        out_spec = pl.BlockSpec((1, HY_R, tt, tc), lambda n, j, i: (n, 0, i, j))
        out_shape = jax.ShapeDtypeStruct((b, HY_R, sub, HY_CH), BF16)
    else:
        in_specs.append(pl.BlockSpec((1, HY_R * tt, tc), lambda n, j, i: (n, i, gcol // tc + j)))
        args.append(gate)
        out_spec = pl.BlockSpec((1, HY_R * tt, tc), lambda n, j, i: (n, i, j))
        out_shape = jax.ShapeDtypeStruct((b, HY_R * sub, HY_CH), BF16)
        scratch = [pltpu.VMEM((tc // LANES, HY_R * tt, LANES), F32)]
    return pl.pallas_call(
        functools.partial(_conv_inv_kernel, gated=gate is not None, tt=tt, tc=tc),
        grid=(b, ncb, sub // tt),
        in_specs=in_specs,
        out_specs=out_spec,
        out_shape=out_shape,
        scratch_shapes=scratch,
        compiler_params=_cparams(("parallel", "parallel", "arbitrary")),
        name="hyena_conv_inv",
    )(*args)


def mixer_hyena(proj, side, dc, tabs, row0, conv_w, conv_b, skip):
    u = conv3(side, conv_w, conv_b)
    uv = conv_fwd(u, 0, dc, tabs, row0, 0)
    z1 = conv_inv(uv, dc, u, HY_CH, u, 0, skip[0])
    uv = conv_fwd(z1, 0, dc, tabs, row0, 1)
    return conv_inv(uv, dc, u, 2 * HY_CH, z1, 0, skip[1], gate=proj, gcol=PM_CZ)


def _merge_kernel(ya_ref, yb_ref, yc_ref, w_ref, g0_ref, g1_ref, g2_ref, o_ref):
    acc = None
    for y_ref, g_ref, n in ((ya_ref, g0_ref, 0), (yb_ref, g1_ref, 1), (yc_ref, g2_ref, 2)):
        t = _sigmoid(g_ref[...].astype(F32)) * jnp.dot(y_ref[...], w_ref[0, n], preferred_element_type=F32)
        acc = t if acc is None else acc + t
    o_ref[...] = acc.astype(o_ref.dtype)


def merge_branches(ya, yb, yc, w_branch, layer, proj2d, d):
    m = ya.shape[0]
    tm, tn = _tile(m, 1024), _tile(d, 512)
    yspec = pl.BlockSpec((tm, BRANCH_W), lambda i, j: (i, 0))

    def gspec(n):
        return pl.BlockSpec((tm, tn), lambda i, j, n=n: (i, (PM_G + n * d) // tn + j))

    return pl.pallas_call(
        _merge_kernel,
        grid=(m // tm, d // tn),
        in_specs=[yspec, yspec, yspec,
                  pl.BlockSpec((1, N_BRANCH, BRANCH_W, tn), lambda i, j: (layer, 0, 0, j)),
                  gspec(0), gspec(1), gspec(2)],
        out_specs=pl.BlockSpec((tm, tn), lambda i, j: (i, j)),
        out_shape=jax.ShapeDtypeStruct((m, d), BF16),
        compiler_params=_cparams(("parallel", "arbitrary")),
        name="branch_merge",
    )(ya, yb, yc, w_branch, proj2d, proj2d, proj2d)


def _out_kernel(a_ref, w_ref, x_ref, mod_ref, o_ref):
    acc = jnp.dot(a_ref[0], w_ref[0], preferred_element_type=F32)
    o_ref[0] = x_ref[0] + mod_ref[0, 2:3, :] * acc


def out_proj(mixed, w_out, layer, x, mod3):
    b, l, d = x.shape
    tm, tn = _tile(l, 1024), _tile(d, 512)
    return pl.pallas_call(
        _out_kernel,
        grid=(b, l // tm, d // tn),
        in_specs=[pl.BlockSpec((1, tm, d), lambda n, i, j: (n, i, 0)),
                  pl.BlockSpec((1, d, tn), lambda n, i, j: (layer, 0, j)),
                  pl.BlockSpec((1, tm, tn), lambda n, i, j: (n, i, j)),
                  pl.BlockSpec((1, 3, tn), lambda n, i, j: (n, 0, j))],
        out_specs=pl.BlockSpec((1, tm, tn), lambda n, i, j: (n, i, j)),
        out_shape=jax.ShapeDtypeStruct((b, l, d), F32),
        compiler_params=_cparams(("parallel", "parallel", "arbitrary")),
        name="out_proj",
    )(mixed, w_out, x, mod3)


def _trunk(x, mods, layers, w_in, w_branch, w_out, dc, slopes, final_g):
    b, l, d = x.shape
    for layer, (lw, mod3) in enumerate(zip(layers, mods)):
        h = norm_mod(x, lw["norm_g"], mod3)
        proj2d, side2d = in_proj(h.reshape(b * l, d), w_in, layer, d)
        proj = proj2d.reshape(b, l, -1)
        side = side2d.reshape(b, l, -1)
        ya = attn_a(proj, side, slopes)
        yb = attn_b(proj, _na_bias_tables(lw["na_rpb"], l // GRID_W))
        taps = hyena_filter_taps(l, lw["hy_w1"], lw["hy_b1"], lw["hy_w2"], lw["hy_b2"], lw["hy_w3"],
                                 lw["hy_freq"], lw["hy_log_decay"])
        tabs, row0 = hyena_filter_spectrum(taps, dc)
        yc = mixer_hyena(proj, side, dc, tabs, row0, lw["conv_w"], lw["conv_b"], lw["hy_skip"])
        mixed = merge_branches(ya.reshape(b * l, -1), yb.reshape(b * l, -1), yc.reshape(b * l, -1),
                               w_branch, layer, proj2d, d)
        x = out_proj(mixed.reshape(b, l, d), w_out, layer, x, mod3)
    return final_norm(x, final_g)


def kernel(x_prompt, x_sample, c_prompt, c_sample, norm_g, w_ada, b_ada, w_in, w_branch, w_out, na_rpb, conv_w, conv_b, hy_w1, hy_b1, hy_w2, hy_b2, hy_w3, hy_freq, hy_log_decay, hy_skip, final_g):
    depth, d = norm_g.shape
    nbp = c_prompt.shape[0]
    c_all = jnp.concatenate([c_prompt, c_sample], axis=0)
    nslope = len(A_GROUPS) * A_HEADS
    slopes = jnp.asarray(2.0 ** (-8.0 * (np.arange(nslope) + 1) / nslope), F32)
    layers, mods_p, mods_s = [], [], []
    for i in range(depth):
        mod3 = ada_mod(c_all, w_ada, b_ada, i).reshape(c_all.shape[0], 3, d)
        mods_p.append(mod3[:nbp])
        mods_s.append(mod3[nbp:])
        layers.append(dict(
            norm_g=norm_g[i], na_rpb=na_rpb[i], conv_w=conv_w[i], conv_b=conv_b[i],
            hy_w1=hy_w1[i], hy_b1=hy_b1[i], hy_w2=hy_w2[i], hy_b2=hy_b2[i], hy_w3=hy_w3[i],
            hy_freq=hy_freq[i], hy_log_decay=hy_log_decay[i], hy_skip=hy_skip[i]))
    w_branch_b, w_out_b = w_branch.astype(BF16), w_out.astype(BF16)
    outs = []
    for x, mods in ((x_prompt, mods_p), (x_sample, mods_s)):
        outs.append(_trunk(x, mods, layers, w_in, w_branch_b, w_out_b, _dft_consts(x.shape[1]), slopes, final_g))
    return tuple(outs)
```

```python
import functools
import math

import jax
import jax.numpy as jnp
import numpy as np
from jax import lax
from jax.experimental import pallas as pl
from jax.experimental.pallas import tpu as pltpu

F32 = jnp.float32
BF16 = jnp.bfloat16

LANES = 128
V7X_VMEM_LIMIT = 56 * 1024 * 1024

HEAD_DIM = 128
GRID_W = 64
BRANCH_W = 1024
N_BRANCH = 3
A_GROUPS = ((128, 1), (512, 4), (2048, 16))
A_HEADS = 8
A_RADIUS = 64
B_HEADS = 8
NA_KR = 8
NA_KC = 16
HY_ORDER = 2
HY_CH = BRANCH_W
HY_BANDS = 16
HY_EMB = 1 + 2 * HY_BANDS
HY_FFN = 64
RMS_EPS = 1e-6
NEG_INF = -1e30
LOG2E = math.log2(math.e)

A_QKV = 3 * len(A_GROUPS) * A_HEADS * HEAD_DIM
B_QKV = 3 * B_HEADS * HEAD_DIM
C_PROJ = (HY_ORDER + 1) * HY_CH
OFF_AZ = A_QKV
OFF_B = OFF_AZ + BRANCH_W
OFF_BZ = OFF_B + B_QKV
OFF_C = OFF_BZ + BRANCH_W
OFF_CZ = OFF_C + C_PROJ
OFF_G = OFF_CZ + BRANCH_W

PM_AZ = 3 * BRANCH_W
PM_B = PM_AZ + BRANCH_W
PM_BZ = PM_B + B_QKV
PM_CZ = PM_BZ + BRANCH_W
PM_G = PM_CZ + BRANCH_W
SIDE_C = 3 * (len(A_GROUPS) - 1) * BRANCH_W
SIDE_COLS = SIDE_C + C_PROJ

NA_QROWS = 4
NA_KROWS = 12
NA_QT = NA_QROWS * GRID_W
NA_KT = NA_KROWS * GRID_W

A_QB = 128

MM_ROWS = 1024
MM_COLS = 512
NORM_ROWS = 256
DFT_ROWS = 256
DFT_COLS = 512


def _cparams(sem):
    return pltpu.CompilerParams(dimension_semantics=sem, vmem_limit_bytes=V7X_VMEM_LIMIT)


def _tile(n, pref):
    t = min(n, pref)
    while n % t:
        t //= 2
    return t


def _sigmoid(z):
    return 0.5 * jnp.tanh(0.5 * z) + 0.5


def _silu(z):
    return z * _sigmoid(z)


def _ada_kernel(c_ref, w_ref, b_ref, o_ref):
    a = _silu(c_ref[...]).astype(BF16)
    o_ref[...] = jnp.dot(a, w_ref[0].astype(BF16), preferred_element_type=F32) + b_ref[0]


def ada_mod(c, w_ada, b_ada, layer):
    nb, d = c.shape
    depth, _, n = w_ada.shape
    tn = _tile(n, MM_COLS)
    return pl.pallas_call(
        _ada_kernel,
        grid=(n // tn,),
        in_specs=[pl.BlockSpec((nb, d), lambda j: (0, 0)),
                  pl.BlockSpec((1, d, tn), lambda j: (layer, 0, j)),
                  pl.BlockSpec((1, 1, tn), lambda j: (layer, 0, j))],
        out_specs=pl.BlockSpec((nb, tn), lambda j: (0, j)),
        out_shape=jax.ShapeDtypeStruct((nb, n), F32),
        compiler_params=_cparams(("arbitrary",)),
        name="ada_mod",
    )(c, w_ada, b_ada.reshape(depth, 1, n))


def _norm_mod_kernel(x_ref, g_ref, mod_ref, o_ref):
    x = x_ref[0]
    y = x * lax.rsqrt(jnp.mean(x * x, axis=-1, keepdims=True) + RMS_EPS) * g_ref[...]
    shift = mod_ref[0, 0:1, :]
    scale = mod_ref[0, 1:2, :]
    o_ref[0] = (y * (1.0 + scale) + shift).astype(o_ref.dtype)


def norm_mod(x, g, mod3):
    b, l, d = x.shape
    tm = _tile(l, NORM_ROWS)
    return pl.pallas_call(
        _norm_mod_kernel,
        grid=(b, l // tm),
        in_specs=[pl.BlockSpec((1, tm, d), lambda i, t: (i, t, 0)),
                  pl.BlockSpec((1, d), lambda i, t: (0, 0)),
                  pl.BlockSpec((1, 3, d), lambda i, t: (i, 0, 0))],
        out_specs=pl.BlockSpec((1, tm, d), lambda i, t: (i, t, 0)),
        out_shape=jax.ShapeDtypeStruct((b, l, d), BF16),
        compiler_params=_cparams(("parallel", "parallel")),
        name="norm_mod",
    )(x, g.reshape(1, d), mod3)


def _final_norm_kernel(x_ref, g_ref, o_ref):
    x = x_ref[0]
    o_ref[0] = x * lax.rsqrt(jnp.mean(x * x, axis=-1, keepdims=True) + RMS_EPS) * g_ref[...]


def final_norm(x, g):
    b, l, d = x.shape
    tm = _tile(l, NORM_ROWS)
    return pl.pallas_call(
        _final_norm_kernel,
        grid=(b, l // tm),
        in_specs=[pl.BlockSpec((1, tm, d), lambda i, t: (i, t, 0)),
                  pl.BlockSpec((1, d), lambda i, t: (0, 0))],
        out_specs=pl.BlockSpec((1, tm, d), lambda i, t: (i, t, 0)),
        out_shape=jax.ShapeDtypeStruct((b, l, d), F32),
        compiler_params=_cparams(("parallel", "parallel")),
        name="final_norm",
    )(x, g.reshape(1, d))


def _mm_kernel(a_ref, w_ref, o_ref, wb_ref):
    @pl.when(pl.program_id(1) == 0)
    def _():
        wb_ref[...] = w_ref[0].astype(BF16)

    o_ref[...] = jnp.dot(a_ref[...], wb_ref[...], preferred_element_type=F32).astype(o_ref.dtype)


def matmul(a, w, layer, n_out, src_block, tn, out_dtype, name):
    m, k = a.shape
    tm = _tile(m, MM_ROWS)
    nrow = m // tm

    def row(j, i):
        return jnp.where(j % 2 == 0, i, nrow - 1 - i)

    return pl.pallas_call(
        _mm_kernel,
        grid=(n_out // tn, nrow),
        in_specs=[pl.BlockSpec((tm, k), lambda j, i: (row(j, i), 0)),
                  pl.BlockSpec((1, k, tn), lambda j, i: (layer, 0, src_block(j)))],
        out_specs=pl.BlockSpec((tm, tn), lambda j, i: (row(j, i), j)),
        out_shape=jax.ShapeDtypeStruct((m, n_out), out_dtype),
        scratch_shapes=[pltpu.VMEM((k, tn), BF16)],
        compiler_params=_cparams(("arbitrary", "arbitrary")),
        name=name,
    )(a, w)


def in_proj(h2d, w_in, layer, d):
    n_main = PM_G + N_BRANCH * d
    tn = _tile(n_main, MM_COLS)
    assert BRANCH_W % tn == 0
    per = BRANCH_W // tn
    ng = len(A_GROUPS)

    def main_src(j):
        return jnp.where(j < 3 * per, (j // per) * ng * per + j % per,
                         jnp.where(j < PM_CZ // tn, j + (OFF_AZ - PM_AZ) // tn, j + (OFF_CZ - PM_CZ) // tn))

    def side_src(j):
        c = j // per
        return jnp.where(j < SIDE_C // tn, ((c // 2) * ng + 1 + c % 2) * per + j % per, j + (OFF_C - SIDE_C) // tn)

    main = matmul(h2d, w_in, layer, n_main, main_src, tn, BF16, "in_proj")
    side = matmul(h2d, w_in, layer, SIDE_COLS, side_src, tn, F32, "in_proj_side")
    return main, side


A_UNROLL = 8
A_STAGE = 4


def _attn_a_kernel(slopes_ref, *refs, seq):
    qkv0 = refs[0:3]
    qkvd = refs[3:9]
    z_ref, o_ref = refs[9], refs[10]
    bias_sc, o_sc, lse_sc, q_st, k_st, v_st = refs[11:]
    h = pl.program_id(1)
    scale = HEAD_DIM ** -0.5 * LOG2E
    ng = len(A_GROUPS)

    for g in list(range(1, ng)) + [0]:
        dil = A_GROUPS[g][1]
        n = seq // dil
        qb = min(A_QB, n)
        slab = min(qb + 2 * A_RADIUS, n)
        nqb = n // qb
        if dil == 1:
            q_ref, k_ref, v_ref = (r.at[0] for r in qkv0)
        else:
            q_ref, k_ref, v_ref = (qkvd[s * 2 + g - 1].at[0] for s in range(3))
        staged = dil > A_STAGE
        if staged:
            quarter = seq // A_STAGE
            for src, dst in ((q_ref, q_st), (k_ref, k_st), (v_ref, v_st)):
                for r4 in range(A_STAGE):
                    dst[pl.ds(r4 * quarter, quarter), :] = src[pl.ds(r4, quarter, stride=A_STAGE), :]
            q_ref, k_ref, v_ref = q_st, k_st, v_st
        stride = dil // A_STAGE if staged else dil
        slope = slopes_ref[g * A_HEADS + h] * (float(dil) * LOG2E)

        col_minus_row = (lax.broadcasted_iota(jnp.int32, (qb, slab), 1)
                         - lax.broadcasted_iota(jnp.int32, (qb, slab), 0))
        for cls, off in enumerate((0, -A_RADIUS, qb - slab)):
            dist = jnp.abs(col_minus_row + off)
            bias_sc[cls, 0:qb, 0:slab] = jnp.where(dist <= A_RADIUS, -slope * dist.astype(F32), NEG_INF)

        def rows_of(ref, start, size, r, dil=dil, staged=staged, stride=stride):
            if dil == 1:
                return ref[pl.ds(pl.multiple_of(start, A_RADIUS), size), :]
            base = (r % A_STAGE) * (seq // A_STAGE) + r // A_STAGE if staged else r
            return ref[pl.ds(base + start * stride, size, stride=stride), :].astype(BF16)

        def body(it, carry, g=g, dil=dil, n=n, qb=qb, slab=slab, nqb=nqb, rows_of=rows_of,
                 q_ref=q_ref, k_ref=k_ref, v_ref=v_ref):
            blks = [it * A_UNROLL + u for u in range(A_UNROLL)]
            res = [blk // nqb for blk in blks]
            q0s = [(blk % nqb) * qb for blk in blks]
            kss = [jnp.clip(q0 - A_RADIUS, 0, n - slab) for q0 in q0s]
            scores = [lax.dot_general(rows_of(q_ref, q0, qb, r), rows_of(k_ref, ks, slab, r),
                                      (((1,), (1,)), ((), ())), preferred_element_type=F32)
                      for q0, ks, r in zip(q0s, kss, res)]
            probs, stats = [], []
            for s, q0 in zip(scores, q0s):
                cls = jnp.where(q0 == 0, 0, jnp.where(q0 + qb == n, 2, 1))
                s = s * scale + bias_sc[cls, 0:qb, 0:slab]
                m = jnp.max(s, axis=-1, keepdims=True)
                p = jnp.exp2(s - m)
                l = jnp.sum(p, axis=-1, keepdims=True)
                probs.append(p.astype(BF16))
                stats.append((m, l))
            outs = [jnp.dot(p, rows_of(v_ref, ks, slab, r), preferred_element_type=F32)
                    for p, ks, r in zip(probs, kss, res)]
            for o, (m, l), q0, r in zip(outs, stats, q0s, res):
                o = o * (1.0 / l)
                lse = jnp.broadcast_to(m + jnp.log(l) * LOG2E, (qb, LANES))
                if dil != 1:
                    rows = pl.ds(q0 * dil + r, qb, stride=dil)
                    o_sc[g - 1, rows, :] = o
                    lse_sc[g - 1, rows, :] = lse
                else:
                    rows = pl.ds(pl.multiple_of(q0, A_RADIUS), qb)
                    l1, l2 = lse_sc[0, rows, :], lse_sc[1, rows, :]
                    mx = jnp.maximum(jnp.maximum(lse, l1), l2)
                    e0, e1, e2 = jnp.exp2(lse - mx), jnp.exp2(l1 - mx), jnp.exp2(l2 - mx)
                    y = (e0 * o + e1 * o_sc[0, rows, :] + e2 * o_sc[1, rows, :]) * (1.0 / (e0 + e1 + e2))
                    o_ref[0, rows, :] = (y * _silu(z_ref[0, rows, :].astype(F32))).astype(o_ref.dtype)
            return carry

        lax.fori_loop(0, (dil * nqb) // A_UNROLL, body, 0)


def attn_a(proj, proj_dil, slopes):
    b, l, _ = proj.shape
    ng = len(A_GROUPS)
    assert ng == 3 and (l // A_QB) % A_UNROLL == 0

    def spec(blk0):
        return pl.BlockSpec((1, l, HEAD_DIM), lambda i, h, blk0=blk0: (i, 0, blk0 + h))

    in_specs = [pl.BlockSpec(memory_space=pltpu.SMEM)]
    in_specs += [spec(s * A_HEADS) for s in range(3)]
    in_specs += [spec(c * A_HEADS) for c in range(3 * (ng - 1))]
    in_specs += [spec(PM_AZ // HEAD_DIM)]
    return pl.pallas_call(
        functools.partial(_attn_a_kernel, seq=l),
        grid=(b, A_HEADS),
        in_specs=in_specs,
        out_specs=pl.BlockSpec((1, l, HEAD_DIM), lambda i, h: (i, 0, h)),
        out_shape=jax.ShapeDtypeStruct((b, l, BRANCH_W), BF16),
        scratch_shapes=[pltpu.VMEM((3, A_QB, A_QB + 2 * A_RADIUS), F32),
                        pltpu.VMEM((ng - 1, l, LANES), F32), pltpu.VMEM((ng - 1, l, LANES), F32),
                        pltpu.VMEM((l, LANES), F32), pltpu.VMEM((l, LANES), F32), pltpu.VMEM((l, LANES), F32)],
        compiler_params=_cparams(("parallel", "parallel")),
        name="attn_dilated",
    )(slopes, proj, proj, proj, *([proj_dil] * (3 * (ng - 1))), proj)


def _na_bias_tables(rpb, rows):
    kr = min(NA_KR, rows)
    qc = np.arange(GRID_W)
    cs = np.clip(qc - NA_KC // 2, 0, GRID_W - NA_KC)
    col_ok = (qc[None, :] >= cs[:, None]) & (qc[None, :] < cs[:, None] + NA_KC)
    dc = np.clip(qc[None, :] - qc[:, None] + NA_KC - 1, 0, 2 * NA_KC - 2)
    dc_onehot = (dc[..., None] == np.arange(2 * NA_KC - 1)).astype(np.float32)
    tabs = []
    for a in (0, 1, rows // NA_QROWS - 1):
        ks = int(np.clip(NA_QROWS * a - NA_QROWS, 0, rows - NA_KROWS))
        rq = NA_QROWS * a + np.arange(NA_QROWS)
        rs = np.clip(rq - kr // 2, 0, rows - kr)
        rk = ks + np.arange(NA_KROWS)
        row_ok = (rk[None, :] >= rs[:, None]) & (rk[None, :] < rs[:, None] + kr)
        dr = np.clip(rk[None, :] - rq[:, None] + NA_KR - 1, 0, 2 * NA_KR - 2)
        dr_onehot = (dr[..., None] == np.arange(2 * NA_KR - 1)).astype(np.float32)
        t = jnp.einsum("qjr,ckd,hrd->hqcjk", dr_onehot, dc_onehot, rpb.astype(F32),
                       precision=lax.Precision.HIGHEST)
        ok = row_ok[:, None, :, None] & col_ok[None, :, None, :]
        tabs.append(jnp.where(ok[None], t * LOG2E, NEG_INF).reshape(-1, NA_QT, NA_KT))
    return jnp.stack(tabs, axis=1)


NA_UNROLL = 4


def _attn_b_kernel(q_ref, k_ref, v_ref, z_ref, bias_ref, o_ref, *, seq):
    rows = seq // GRID_W
    nblk = rows // NA_QROWS
    scale = HEAD_DIM ** -0.5 * LOG2E

    def body(it, carry):
        blks = [it * NA_UNROLL + u for u in range(NA_UNROLL)]
        ks_rows = [jnp.clip(NA_QROWS * a - NA_QROWS, 0, rows - NA_KROWS) for a in blks]
        qrows = [pl.ds(pl.multiple_of(a * NA_QT, NA_QT), NA_QT) for a in blks]
        krows = [pl.ds(pl.multiple_of(ks * GRID_W, GRID_W), NA_KT) for ks in ks_rows]
        scores = [lax.dot_general(q_ref[0, qr, :], k_ref[0, kr, :], (((1,), (1,)), ((), ())),
                                  preferred_element_type=F32) for qr, kr in zip(qrows, krows)]
        probs, sums = [], []
        for s, a, ks in zip(scores, blks, ks_rows):
            s = s * scale + bias_ref[0, (NA_QROWS * a - ks) // NA_QROWS]
            p = jnp.exp2(s - jnp.max(s, axis=-1, keepdims=True))
            sums.append(jnp.sum(p, axis=-1, keepdims=True))
            probs.append(p.astype(BF16))
        outs = [jnp.dot(p, v_ref[0, kr, :], preferred_element_type=F32) for p, kr in zip(probs, krows)]
        for o, l, qr in zip(outs, sums, qrows):
            o_ref[0, qr, :] = (o * (1.0 / l) * _silu(z_ref[0, qr, :].astype(F32))).astype(o_ref.dtype)
        return carry

    lax.fori_loop(0, nblk // NA_UNROLL, body, 0)


def attn_b(proj, bias):
    b, l, _ = proj.shape
    assert l % (NA_QT * NA_UNROLL) == 0 and l // GRID_W >= NA_KROWS

    def col_spec(off):
        return pl.BlockSpec((1, l, HEAD_DIM), lambda h, i, off=off: (i, 0, off // HEAD_DIM + h))

    return pl.pallas_call(
        functools.partial(_attn_b_kernel, seq=l),
        grid=(B_HEADS, b),
        in_specs=[col_spec(PM_B), col_spec(PM_B + B_HEADS * HEAD_DIM),
                  col_spec(PM_B + 2 * B_HEADS * HEAD_DIM), col_spec(PM_BZ),
                  pl.BlockSpec((1, 3, NA_QT, NA_KT), lambda h, i: (h, 0, 0, 0))],
        out_specs=pl.BlockSpec((1, l, HEAD_DIM), lambda h, i: (i, 0, h)),
        out_shape=jax.ShapeDtypeStruct((b, l, BRANCH_W), BF16),
        compiler_params=_cparams(("parallel", "parallel")),
        name="attn_neighbourhood",
    )(proj, proj, proj, proj, bias)


HY_R = 4
HY_ND = 2 * HY_R - 1
HY_ROW0_M = 8
HY_ROW0 = 2 * HY_ROW0_M
HY_EMB_PAD = 64
HY_RC = 16


def _shift_down(x):
    t = lax.broadcasted_iota(jnp.int32, x.shape, 0)
    return jnp.where(t == 0, 0.0, pltpu.roll(x, 1, axis=0))


def _shift_up(x):
    n = x.shape[0]
    t = lax.broadcasted_iota(jnp.int32, x.shape, 0)
    return jnp.where(t == n - 1, 0.0, pltpu.roll(x, n - 1, axis=0))


def _conv3_kernel(u_ref, w_ref, b_ref, o_ref, *, sub):
    xs = [u_ref[0, pl.ds(p, sub, stride=HY_R), :] for p in range(HY_R)]
    w0, w1, w2, bias = w_ref[0:1, :], w_ref[1:2, :], w_ref[2:3, :], b_ref[...]
    for p in range(HY_R):
        prev = xs[p - 1] if p > 0 else _shift_down(xs[HY_R - 1])
        nxt = xs[p + 1] if p < HY_R - 1 else _shift_up(xs[0])
        o_ref[0, p] = (w0 * prev + w1 * xs[p] + w2 * nxt + bias).astype(o_ref.dtype)


def conv3(side, conv_w, conv_b):
    b, l, _ = side.shape
    sub = l // HY_R
    return pl.pallas_call(
        functools.partial(_conv3_kernel, sub=sub),
        grid=(b, C_PROJ // LANES),
        in_specs=[pl.BlockSpec((1, l, LANES), lambda i, j: (i, 0, SIDE_C // LANES + j)),
                  pl.BlockSpec((3, LANES), lambda i, j: (0, j)),
                  pl.BlockSpec((1, LANES), lambda i, j: (0, j))],
        out_specs=pl.BlockSpec((1, HY_R, sub, LANES), lambda i, j: (i, 0, 0, j)),
        out_shape=jax.ShapeDtypeStruct((b, HY_R, sub, C_PROJ), BF16),
        compiler_params=_cparams(("parallel", "parallel")),
        name="hyena_conv3",
    )(side, conv_w, conv_b.reshape(1, C_PROJ))


def _filter_kernel(feats_ref, w1_ref, b1_ref, w2_ref, b2_ref, freq_ref, w3_ref, dec_ref, o_ref, hid_ref,
                   *, tc):
    j = pl.program_id(0)
    hp = lax.Precision.HIGHEST

    @pl.when(j == 0)
    def _():
        h1 = jnp.sin(freq_ref[0:1, :] * (jnp.dot(feats_ref[...], w1_ref[...], precision=hp,
                                                 preferred_element_type=F32) + b1_ref[...]))
        hid_ref[...] = jnp.sin(freq_ref[1:2, :] * (jnp.dot(h1, w2_ref[...], precision=hp,
                                                           preferred_element_type=F32) + b2_ref[...]))

    raw = jnp.dot(hid_ref[...].astype(BF16), w3_ref[...].astype(BF16), preferred_element_type=F32)
    t_norm = feats_ref[:, 0:1]
    hfil = raw * jnp.exp(-t_norm * jnp.exp(dec_ref[...]))
    hfil = hfil * (1.0 / (jnp.sum(jnp.abs(hfil), axis=0, keepdims=True) + 1e-6))
    dropped_row = ((j * tc) // HY_CH) % 2 - 1
    t = lax.broadcasted_iota(jnp.int32, hfil.shape, 0)
    o_ref[...] = jnp.where(t == dropped_row, 0.0, hfil).astype(o_ref.dtype)


def hyena_filter_taps(seq, w1, b1, w2, b2, w3, freq, log_decay):
    t = jnp.concatenate([jnp.arange(p, seq, HY_R) for p in range(HY_R)]).astype(F32)
    t_norm = t / (seq - 1)
    ang = (2.0 * math.pi / seq) * t
    bands = jnp.linspace(1e-4, HY_BANDS - 1, HY_BANDS, dtype=F32)
    feats = jnp.concatenate([t_norm[:, None], jnp.cos(ang[:, None] * bands), -jnp.sin(ang[:, None] * bands)], axis=-1)
    emb = HY_EMB_PAD
    feats = jnp.pad(feats, ((0, 0), (0, emb - HY_EMB)))
    w1p = jnp.pad(w1, ((0, emb - HY_EMB), (0, 0)))
    ncol = HY_ORDER * 2 * HY_CH
    tc = 256
    full = lambda shape: pl.BlockSpec(shape, lambda j: (0, 0))
    return pl.pallas_call(
        functools.partial(_filter_kernel, tc=tc),
        grid=(ncol // tc,),
        in_specs=[full((seq, emb)), full((emb, HY_FFN)), full((1, HY_FFN)), full((HY_FFN, HY_FFN)),
                  full((1, HY_FFN)), full((2, HY_FFN)),
                  pl.BlockSpec((HY_FFN, tc), lambda j: (0, j)),
                  pl.BlockSpec((1, tc), lambda j: (0, j))],
        out_specs=pl.BlockSpec((seq, tc), lambda j: (0, j)),
        out_shape=jax.ShapeDtypeStruct((seq, ncol), BF16),
        scratch_shapes=[pltpu.VMEM((seq, HY_FFN), F32)],
        compiler_params=_cparams(("arbitrary",)),
        name="hyena_filter_taps",
    )(feats, w1p, b1.reshape(1, HY_FFN), w2, b2.reshape(1, HY_FFN), freq, w3, log_decay.reshape(1, ncol))


def _dft_consts(seq):
    sub = seq // HY_R
    k = jnp.arange(sub, dtype=jnp.int32)
    ang = ((k[:, None] * k[None, :]) % (2 * sub)).astype(F32) * (math.pi / sub)
    cmat = jnp.cos(ang)
    smat = jnp.sin(ang)
    sign = (1 - 2 * (k % 2)).astype(F32)
    smat = jnp.where((k == 0)[:, None], sign[None, :], smat)
    tw = k.astype(F32) * (math.pi / sub)
    cw = jnp.broadcast_to(jnp.cos(tw)[:, None], (sub, LANES))
    sw = jnp.broadcast_to(jnp.sin(tw)[:, None], (sub, LANES))
    return dict(cmat=cmat.astype(BF16), smat=smat.astype(BF16), smat_t=smat.T.astype(BF16), cw=cw, sw=sw)


def _residue_transforms(c_ref, s_ref, x_ref):
    cm, sm = c_ref[...], s_ref[...]
    dot = lambda a, b: jnp.dot(a, b, preferred_element_type=F32)
    return [(dot(cm, x_ref[p]), dot(sm, x_ref[p])) for p in range(HY_R)]


def _filter_spec_kernel(c_ref, s_ref, cw_ref, sw_ref, xf_ref, xb_ref, tab_ref, row0_ref, *, seq, tc):
    i = pl.program_id(2)
    fwd = _residue_transforms(c_ref, s_ref, xf_ref)
    bwd = _residue_transforms(c_ref, s_ref, xb_ref)
    c = jnp.tile(cw_ref[...], (1, tc // LANES))
    s = jnp.tile(sw_ref[...], (1, tc // LANES))
    inv_n = 1.0 / (2 * seq)
    sc = HY_R * 2.0 * inv_n
    mid = HY_R - 1

    def put(d, tr, ti):
        tab_ref[0, 2 * (d + mid)] = sc * tr
        tab_ref[0, 2 * (d + mid) + 1] = sc * ti

    put(0, fwd[0][0] + bwd[0][0], bwd[0][1] - fwd[0][1])
    for d in range(1, HY_R):
        (pf, qf), (pb, qb) = fwd[HY_R - d], bwd[d]
        put(d, c * pf - s * qf + pb, qb - c * qf - s * pf)
        (pf, qf), (pb, qb) = fwd[d], bwd[HY_R - d]
        put(-d, pf + c * pb - s * qb, c * qb + s * pb - qf)

    @pl.when(i == 0)
    def _():
        row0_ref[0] = jnp.zeros(row0_ref.shape[1:], F32)
        rn = HY_R * inv_n

        def put0(d, bin0, bin_m):
            row0_ref[0, d + mid:d + mid + 1, :] = rn * bin0
            row0_ref[0, HY_ROW0_M + d + mid:HY_ROW0_M + d + mid + 1, :] = rn * bin_m

        put0(0, fwd[0][0][0:1] + bwd[0][0][0:1], fwd[0][1][0:1] + bwd[0][1][0:1])
        for d in range(1, HY_R):
            put0(d, fwd[HY_R - d][0][0:1] + bwd[d][0][0:1], bwd[d][1][0:1] - fwd[HY_R - d][1][0:1])
            put0(-d, fwd[d][0][0:1] + bwd[HY_R - d][0][0:1], fwd[d][1][0:1] - bwd[HY_R - d][1][0:1])


def hyena_filter_spectrum(taps, dc):
    seq = taps.shape[0]
    sub = seq // HY_R
    taps = taps.reshape(HY_R, sub, taps.shape[1])
    tf, tc = _tile(sub, DFT_ROWS), DFT_COLS // 2
    ncb = HY_CH // tc
    mspec = pl.BlockSpec((tf, sub), lambda o, j, i: (i, 0))
    tspec = pl.BlockSpec((tf, LANES), lambda o, j, i: (i, 0))
    return pl.pallas_call(
        functools.partial(_filter_spec_kernel, seq=seq, tc=tc),
        grid=(HY_ORDER, ncb, sub // tf),
        in_specs=[mspec, mspec, tspec, tspec,
                  pl.BlockSpec((HY_R, sub, tc), lambda o, j, i: (0, 0, (o * 2) * ncb + j)),
                  pl.BlockSpec((HY_R, sub, tc), lambda o, j, i: (0, 0, (o * 2 + 1) * ncb + j))],
        out_specs=[pl.BlockSpec((1, 2 * HY_ND, tf, tc), lambda o, j, i: (o, 0, i, j)),
                   pl.BlockSpec((1, HY_ROW0, tc), lambda o, j, i: (o, 0, j))],
        out_shape=[jax.ShapeDtypeStruct((HY_ORDER, 2 * HY_ND, sub, HY_CH), F32),
                   jax.ShapeDtypeStruct((HY_ORDER, HY_ROW0, HY_CH), F32)],
        compiler_params=_cparams(("parallel", "parallel", "arbitrary")),
        name="hyena_filter_spectrum",
    )(dc["cmat"], dc["smat"], dc["cw"], dc["sw"], taps, taps)


def _conv_fwd_kernel(c_ref, s_ref, z_ref, tab_ref, row0_ref, uv_ref, pq_ref, *, tf, tc):
    i = pl.program_id(1)
    cm, sm = c_ref[...], s_ref[...]
    dot = lambda a, b: jnp.dot(a, b, preferred_element_type=F32)
    mid = HY_R - 1
    halves = [slice(0, tc // 2), slice(tc // 2, tc)]

    def transforms(lanes):
        for p in range(HY_R):
            x = z_ref[0, p, :, lanes]
            pq_ref[2 * p, :, lanes] = dot(cm, x)
            pq_ref[2 * p + 1, :, lanes] = dot(sm, x)

    def products(lanes):
        for r in range(tf // HY_RC):
            rows = slice(r * HY_RC, (r + 1) * HY_RC)
            for q in range(HY_R):
                u = v = None
                for p in range(HY_R):
                    pp, qp = pq_ref[2 * p, rows, lanes], pq_ref[2 * p + 1, rows, lanes]
                    tr = tab_ref[0, 2 * (p - q + mid), rows, lanes]
                    ti = tab_ref[0, 2 * (p - q + mid) + 1, rows, lanes]
                    du, dv = pp * tr + qp * ti, qp * tr - pp * ti
                    u, v = (du, dv) if u is None else (u + du, v + dv)
                uv_ref[0, 2 * q, rows, lanes] = u.astype(uv_ref.dtype)
                uv_ref[0, 2 * q + 1, rows, lanes] = v.astype(uv_ref.dtype)

    transforms(halves[0])
    transforms(halves[1])
    products(halves[0])
    products(halves[1])

    @pl.when(i == 0)
    def _():
        head = slice(0, HY_RC)
        first = lax.broadcasted_iota(jnp.int32, (HY_RC, 1), 0) == 0
        for q in range(HY_R):
            u0 = v0 = None
            for p in range(HY_R):
                du = pq_ref[2 * p, 0:1, :] * row0_ref[0, p - q + mid:p - q + mid + 1, :]
                dv = pq_ref[2 * p + 1, 0:1, :] * row0_ref[0, HY_ROW0_M + p - q + mid:HY_ROW0_M + p - q + mid + 1, :]
                u0, v0 = (du, dv) if u0 is None else (u0 + du, v0 + dv)
            uv_ref[0, 2 * q, head, :] = jnp.where(first, u0, uv_ref[0, 2 * q, head, :].astype(F32)).astype(uv_ref.dtype)
            uv_ref[0, 2 * q + 1, head, :] = jnp.where(first, v0, uv_ref[0, 2 * q + 1, head, :].astype(F32)).astype(uv_ref.dtype)


def conv_fwd(z, zcol, dc, tabs, row0, order):
    b, _, sub, _ = z.shape
    tf, tc = _tile(sub, DFT_ROWS), DFT_COLS
    ncb = HY_CH // tc
    mspec = pl.BlockSpec((tf, sub), lambda j, i, n: (i, 0))
    return pl.pallas_call(
        functools.partial(_conv_fwd_kernel, tf=tf, tc=tc),
        grid=(ncb, sub // tf, b),
        in_specs=[mspec, mspec,
                  pl.BlockSpec((1, HY_R, sub, tc), lambda j, i, n: (n, 0, 0, zcol // tc + j)),
                  pl.BlockSpec((1, 2 * HY_ND, tf, tc), lambda j, i, n: (order, 0, i, j)),
                  pl.BlockSpec((1, HY_ROW0, tc), lambda j, i, n: (order, 0, j))],
        out_specs=pl.BlockSpec((1, 2 * HY_R, tf, tc), lambda j, i, n: (n, 0, i, j)),
        out_shape=jax.ShapeDtypeStruct((b, 2 * HY_R, sub, HY_CH), BF16),
        scratch_shapes=[pltpu.VMEM((2 * HY_R, tf, tc), F32)],
        compiler_params=_cparams(("parallel", "parallel", "arbitrary")),
        name="hyena_conv_fwd",
    )(dc["cmat"], dc["smat"], z, tabs, row0)


def _conv_inv_kernel(c_ref, st_ref, uv_ref, x_ref, zp_ref, skip_ref, *rest, gated, tt, tc):
    cm, sm = c_ref[...], st_ref[...]
    dot = lambda a, b: jnp.dot(a, b, preferred_element_type=F32)
    skip = skip_ref[0]
    parts = []
    for q in range(HY_R):
        y = dot(cm, uv_ref[0, 2 * q]) + dot(sm, uv_ref[0, 2 * q + 1])
        parts.append(x_ref[0, q].astype(F32) * (y + zp_ref[0, q].astype(F32) * skip))
    if not gated:
        o_ref = rest[0]
        for q in range(HY_R):
            o_ref[0, q] = parts[q].astype(o_ref.dtype)
        return
    gate_ref, o_ref, il_ref = rest
    for s in range(tc // LANES):
        lanes = slice(s * LANES, (s + 1) * LANES)
        for q in range(HY_R):
            il_ref[s, pl.ds(q, tt, stride=HY_R), :] = parts[q][:, lanes]
    for s in range(tc // LANES):
        lanes = slice(s * LANES, (s + 1) * LANES)
        o_ref[0, :, lanes] = (il_ref[s] * _silu(gate_ref[0, :, lanes].astype(F32))).astype(o_ref.dtype)


def conv_inv(uv, dc, xmul, xcol, zprev, zcol, skip, gate=None, gcol=0):
    b, _, sub, _ = uv.shape
    tt, tc = _tile(sub, DFT_ROWS), DFT_COLS
    ncb = HY_CH // tc
    mspec = pl.BlockSpec((tt, sub), lambda n, j, i: (i, 0))

    def tile_spec(col):
        return pl.BlockSpec((1, HY_R, tt, tc), lambda n, j, i, col=col: (n, 0, i, col // tc + j))

    in_specs = [mspec, mspec, pl.BlockSpec((1, 2 * HY_R, sub, tc), lambda n, j, i: (n, 0, 0, j)),
                tile_spec(xcol), tile_spec(zcol), pl.BlockSpec((1, 1, tc), lambda n, j, i: (0, 0, j))]
    args = [dc["cmat"], dc["smat_t"], uv, xmul, zprev, skip.reshape(1, 1, HY_CH)]
    scratch = []
    if gate is None:
        out_spec = pl.BlockSpec((1, HY_R, tt, tc), lambda n, j, i: (n, 0, i, j))
        out_shape = jax.ShapeDtypeStruct((b, HY_R, sub, HY_CH), BF16)
    else:
        in_specs.append(pl.BlockSpec((1, HY_R * tt, tc), lambda n, j, i: (n, i, gcol // tc + j)))
        args.append(gate)
        out_spec = pl.BlockSpec((1, HY_R * tt, tc), lambda n, j, i: (n, i, j))
        out_shape = jax.ShapeDtypeStruct((b, HY_R * sub, HY_CH), BF16)
        scratch = [pltpu.VMEM((tc // LANES, HY_R * tt, LANES), F32)]
    return pl.pallas_call(
        functools.partial(_conv_inv_kernel, gated=gate is not None, tt=tt, tc=tc),
        grid=(b, ncb, sub // tt),
        in_specs=in_specs,
        out_specs=out_spec,
        out_shape=out_shape,
        scratch_shapes=scratch,
        compiler_params=_cparams(("parallel", "parallel", "arbitrary")),
        name="hyena_conv_inv",
    )(*args)


def mixer_hyena(proj, side, dc, tabs, row0, conv_w, conv_b, skip):
    u = conv3(side, conv_w, conv_b)
    uv = conv_fwd(u, 0, dc, tabs, row0, 0)
    z1 = conv_inv(uv, dc, u, HY_CH, u, 0, skip[0])
    uv = conv_fwd(z1, 0, dc, tabs, row0, 1)
    return conv_inv(uv, dc, u, 2 * HY_CH, z1, 0, skip[1], gate=proj, gcol=PM_CZ)


def _merge_kernel(ya_ref, yb_ref, yc_ref, w_ref, g0_ref, g1_ref, g2_ref, o_ref):
    acc = None
    for y_ref, g_ref, n in ((ya_ref, g0_ref, 0), (yb_ref, g1_ref, 1), (yc_ref, g2_ref, 2)):
        t = _sigmoid(g_ref[...].astype(F32)) * jnp.dot(y_ref[...], w_ref[0, n], preferred_element_type=F32)
        acc = t if acc is None else acc + t
    o_ref[...] = acc.astype(o_ref.dtype)


def merge_branches(ya, yb, yc, w_branch, layer, proj2d, d):
    m = ya.shape[0]
    tm, tn = _tile(m, MM_ROWS), _tile(d, MM_COLS)
    yspec = pl.BlockSpec((tm, BRANCH_W), lambda i, j: (i, 0))

    def gspec(n):
        return pl.BlockSpec((tm, tn), lambda i, j, n=n: (i, (PM_G + n * d) // tn + j))

    return pl.pallas_call(
        _merge_kernel,
        grid=(m // tm, d // tn),
        in_specs=[yspec, yspec, yspec,
                  pl.BlockSpec((1, N_BRANCH, BRANCH_W, tn), lambda i, j: (layer, 0, 0, j)),
                  gspec(0), gspec(1), gspec(2)],
        out_specs=pl.BlockSpec((tm, tn), lambda i, j: (i, j)),
        out_shape=jax.ShapeDtypeStruct((m, d), BF16),
        compiler_params=_cparams(("parallel", "arbitrary")),
        name="branch_merge",
    )(ya, yb, yc, w_branch, proj2d, proj2d, proj2d)


def _out_kernel(a_ref, w_ref, x_ref, mod_ref, o_ref):
    acc = jnp.dot(a_ref[0], w_ref[0], preferred_element_type=F32)
    o_ref[0] = x_ref[0] + mod_ref[0, 2:3, :] * acc


def out_proj(mixed, w_out, layer, x, mod3):
    b, l, d = x.shape
    tm, tn = _tile(l, MM_ROWS), _tile(d, MM_COLS)
    return pl.pallas_call(
        _out_kernel,
        grid=(b, l // tm, d // tn),
        in_specs=[pl.BlockSpec((1, tm, d), lambda n, i, j: (n, i, 0)),
                  pl.BlockSpec((1, d, tn), lambda n, i, j: (layer, 0, j)),
                  pl.BlockSpec((1, tm, tn), lambda n, i, j: (n, i, j)),
                  pl.BlockSpec((1, 3, tn), lambda n, i, j: (n, 0, j))],
        out_specs=pl.BlockSpec((1, tm, tn), lambda n, i, j: (n, i, j)),
        out_shape=jax.ShapeDtypeStruct((b, l, d), F32),
        compiler_params=_cparams(("parallel", "parallel", "arbitrary")),
        name="out_proj",
    )(mixed, w_out, x, mod3)


def _trunk(x, mods, layers, w_in, w_branch, w_out, dc, slopes, final_g):
    b, l, d = x.shape
    for layer, (lw, mod3) in enumerate(zip(layers, mods)):
        h = norm_mod(x, lw["norm_g"], mod3)
        proj2d, side2d = in_proj(h.reshape(b * l, d), w_in, layer, d)
        proj = proj2d.reshape(b, l, -1)
        side = side2d.reshape(b, l, -1)
        ya = attn_a(proj, side, slopes)
        yb = attn_b(proj, _na_bias_tables(lw["na_rpb"], l // GRID_W))
        taps = hyena_filter_taps(l, lw["hy_w1"], lw["hy_b1"], lw["hy_w2"], lw["hy_b2"], lw["hy_w3"],
                                 lw["hy_freq"], lw["hy_log_decay"])
        tabs, row0 = hyena_filter_spectrum(taps, dc)
        yc = mixer_hyena(proj, side, dc, tabs, row0, lw["conv_w"], lw["conv_b"], lw["hy_skip"])
        mixed = merge_branches(ya.reshape(b * l, -1), yb.reshape(b * l, -1), yc.reshape(b * l, -1),
                               w_branch, layer, proj2d, d)
        x = out_proj(mixed.reshape(b, l, d), w_out, layer, x, mod3)
    return final_norm(x, final_g)


def kernel(x_prompt, x_sample, c_prompt, c_sample, norm_g, w_ada, b_ada, w_in, w_branch, w_out, na_rpb, conv_w, conv_b, hy_w1, hy_b1, hy_w2, hy_b2, hy_w3, hy_freq, hy_log_decay, hy_skip, final_g):
    depth, d = norm_g.shape
    nbp = c_prompt.shape[0]
    c_all = jnp.concatenate([c_prompt, c_sample], axis=0)
    nslope = len(A_GROUPS) * A_HEADS
    slopes = jnp.asarray(2.0 ** (-8.0 * (np.arange(nslope) + 1) / nslope), F32)
    layers, mods_p, mods_s = [], [], []
    for i in range(depth):
        mod3 = ada_mod(c_all, w_ada, b_ada, i).reshape(c_all.shape[0], 3, d)
        mods_p.append(mod3[:nbp])
        mods_s.append(mod3[nbp:])
        layers.append(dict(
            norm_g=norm_g[i], na_rpb=na_rpb[i], conv_w=conv_w[i], conv_b=conv_b[i],
            hy_w1=hy_w1[i], hy_b1=hy_b1[i], hy_w2=hy_w2[i], hy_b2=hy_b2[i], hy_w3=hy_w3[i],
            hy_freq=hy_freq[i], hy_log_decay=hy_log_decay[i], hy_skip=hy_skip[i]))
    w_branch_b, w_out_b = w_branch.astype(BF16), w_out.astype(BF16)
    outs = []
    for x, mods in ((x_prompt, mods_p), (x_sample, mods_s)):
        outs.append(_trunk(x, mods, layers, w_in, w_branch_b, w_out_b, _dft_consts(x.shape[1]), slopes, final_g))
    return tuple(outs)
```

```python
import functools
import math

import jax
import jax.numpy as jnp
import numpy as np
from jax import lax
from jax.experimental import pallas as pl
from jax.experimental.pallas import tpu as pltpu

F32 = jnp.float32
BF16 = jnp.bfloat16

LANES = 128
V7X_VMEM_LIMIT = 56 * 1024 * 1024

HEAD_DIM = 128
GRID_W = 64
BRANCH_W = 1024
N_BRANCH = 3
A_GROUPS = ((128, 1), (512, 4), (2048, 16))
A_HEADS = 8
A_RADIUS = 64
B_HEADS = 8
NA_KR = 8
NA_KC = 16
HY_ORDER = 2
HY_CH = BRANCH_W
HY_BANDS = 16
HY_EMB = 1 + 2 * HY_BANDS
HY_FFN = 64
RMS_EPS = 1e-6
NEG_INF = -1e30
LOG2E = math.log2(math.e)

A_QKV = 3 * len(A_GROUPS) * A_HEADS * HEAD_DIM
B_QKV = 3 * B_HEADS * HEAD_DIM
C_PROJ = (HY_ORDER + 1) * HY_CH
OFF_AZ = A_QKV
OFF_B = OFF_AZ + BRANCH_W
OFF_BZ = OFF_B + B_QKV
OFF_C = OFF_BZ + BRANCH_W
OFF_CZ = OFF_C + C_PROJ
OFF_G = OFF_CZ + BRANCH_W

PM_AZ = 3 * BRANCH_W
PM_B = PM_AZ + BRANCH_W
PM_BZ = PM_B + B_QKV
PM_CZ = PM_BZ + BRANCH_W
PM_G = PM_CZ + BRANCH_W
SIDE_C = 3 * (len(A_GROUPS) - 1) * BRANCH_W
SIDE_COLS = SIDE_C + C_PROJ

NA_QROWS = 4
NA_KROWS = 12
NA_QT = NA_QROWS * GRID_W
NA_KT = NA_KROWS * GRID_W

A_QB = 128


def _cparams(sem):
    return pltpu.CompilerParams(dimension_semantics=sem, vmem_limit_bytes=V7X_VMEM_LIMIT)


def _tile(n, pref):
    t = min(n, pref)
    while n % t:
        t //= 2
    return t


def _sigmoid(z):
    return 0.5 * jnp.tanh(0.5 * z) + 0.5


def _silu(z):
    return z * _sigmoid(z)


def _ada_kernel(c_ref, w_ref, b_ref, o_ref):
    a = _silu(c_ref[...]).astype(BF16)
    o_ref[...] = jnp.dot(a, w_ref[0].astype(BF16), preferred_element_type=F32) + b_ref[0]


def ada_mod(c, w_ada, b_ada, layer):
    nb, d = c.shape
    depth, _, n = w_ada.shape
    tn = _tile(n, 512)
    return pl.pallas_call(
        _ada_kernel,
        grid=(n // tn,),
        in_specs=[pl.BlockSpec((nb, d), lambda j: (0, 0)),
                  pl.BlockSpec((1, d, tn), lambda j: (layer, 0, j)),
                  pl.BlockSpec((1, 1, tn), lambda j: (layer, 0, j))],
        out_specs=pl.BlockSpec((nb, tn), lambda j: (0, j)),
        out_shape=jax.ShapeDtypeStruct((nb, n), F32),
        compiler_params=_cparams(("arbitrary",)),
        name="ada_mod",
    )(c, w_ada, b_ada.reshape(depth, 1, n))


def _norm_mod_kernel(x_ref, g_ref, mod_ref, o_ref):
    x = x_ref[0]
    y = x * lax.rsqrt(jnp.mean(x * x, axis=-1, keepdims=True) + RMS_EPS) * g_ref[...]
    shift = mod_ref[0, 0:1, :]
    scale = mod_ref[0, 1:2, :]
    o_ref[0] = (y * (1.0 + scale) + shift).astype(o_ref.dtype)


def norm_mod(x, g, mod3):
    b, l, d = x.shape
    tm = _tile(l, 256)
    return pl.pallas_call(
        _norm_mod_kernel,
        grid=(b, l // tm),
        in_specs=[pl.BlockSpec((1, tm, d), lambda i, t: (i, t, 0)),
                  pl.BlockSpec((1, d), lambda i, t: (0, 0)),
                  pl.BlockSpec((1, 3, d), lambda i, t: (i, 0, 0))],
        out_specs=pl.BlockSpec((1, tm, d), lambda i, t: (i, t, 0)),
        out_shape=jax.ShapeDtypeStruct((b, l, d), BF16),
        compiler_params=_cparams(("parallel", "parallel")),
        name="norm_mod",
    )(x, g.reshape(1, d), mod3)


def _final_norm_kernel(x_ref, g_ref, o_ref):
    x = x_ref[0]
    o_ref[0] = x * lax.rsqrt(jnp.mean(x * x, axis=-1, keepdims=True) + RMS_EPS) * g_ref[...]


def final_norm(x, g):
    b, l, d = x.shape
    tm = _tile(l, 256)
    return pl.pallas_call(
        _final_norm_kernel,
        grid=(b, l // tm),
        in_specs=[pl.BlockSpec((1, tm, d), lambda i, t: (i, t, 0)),
                  pl.BlockSpec((1, d), lambda i, t: (0, 0))],
        out_specs=pl.BlockSpec((1, tm, d), lambda i, t: (i, t, 0)),
        out_shape=jax.ShapeDtypeStruct((b, l, d), F32),
        compiler_params=_cparams(("parallel", "parallel")),
        name="final_norm",
    )(x, g.reshape(1, d))


def _mm_kernel(a_ref, w_ref, o_ref, wb_ref):
    @pl.when(pl.program_id(1) == 0)
    def _():
        wb_ref[...] = w_ref[0].astype(BF16)

    o_ref[...] = jnp.dot(a_ref[...], wb_ref[...], preferred_element_type=F32).astype(o_ref.dtype)


def matmul(a, w, layer, n_out, src_block, tn, out_dtype, name):
    m, k = a.shape
    tm = _tile(m, 1024)
    nrow = m // tm

    def row(j, i):
        return jnp.where(j % 2 == 0, i, nrow - 1 - i)

    return pl.pallas_call(
        _mm_kernel,
        grid=(n_out // tn, nrow),
        in_specs=[pl.BlockSpec((tm, k), lambda j, i: (row(j, i), 0)),
                  pl.BlockSpec((1, k, tn), lambda j, i: (layer, 0, src_block(j)))],
        out_specs=pl.BlockSpec((tm, tn), lambda j, i: (row(j, i), j)),
        out_shape=jax.ShapeDtypeStruct((m, n_out), out_dtype),
        scratch_shapes=[pltpu.VMEM((k, tn), BF16)],
        compiler_params=_cparams(("arbitrary", "arbitrary")),
        name=name,
    )(a, w)


def in_proj(h2d, w_in, layer, d):
    n_main = PM_G + N_BRANCH * d
    tn = _tile(n_main, 512)
    assert BRANCH_W % tn == 0
    per = BRANCH_W // tn
    ng = len(A_GROUPS)

    def main_src(j):
        return jnp.where(j < 3 * per, (j // per) * ng * per + j % per,
                         jnp.where(j < PM_CZ // tn, j + (OFF_AZ - PM_AZ) // tn, j + (OFF_CZ - PM_CZ) // tn))

    def side_src(j):
        c = j // per
        return jnp.where(j < SIDE_C // tn, ((c // 2) * ng + 1 + c % 2) * per + j % per, j + (OFF_C - SIDE_C) // tn)

    main = matmul(h2d, w_in, layer, n_main, main_src, tn, BF16, "in_proj")
    side = matmul(h2d, w_in, layer, SIDE_COLS, side_src, tn, F32, "in_proj_side")
    return main, side


A_UNROLL = 8
A_STAGE = 4


def _attn_a_kernel(slopes_ref, *refs, seq):
    qkv0 = refs[0:3]
    qkvd = refs[3:9]
    z_ref, o_ref = refs[9], refs[10]
    bias_sc, o_sc, lse_sc, k_st, v_st = refs[11:]
    h = pl.program_id(1)
    scale = HEAD_DIM ** -0.5 * LOG2E
    ng = len(A_GROUPS)

    for g in list(range(1, ng)) + [0]:
        dil = A_GROUPS[g][1]
        n = seq // dil
        qb = min(A_QB, n)
        slab = min(qb + 2 * A_RADIUS, n)
        nqb = n // qb
        if dil == 1:
            q_ref, k_ref, v_ref = (r.at[0] for r in qkv0)
        else:
            q_ref, k_ref, v_ref = (qkvd[s * 2 + g - 1].at[0] for s in range(3))
        staged = dil > A_STAGE
        if staged:
            quarter = seq // A_STAGE
            for src, dst in ((k_ref, k_st), (v_ref, v_st)):
                for r4 in range(A_STAGE):
                    dst[pl.ds(r4 * quarter, quarter), :] = src[pl.ds(r4, quarter, stride=A_STAGE), :]
            k_ref, v_ref = k_st, v_st
        kv_stride = dil // A_STAGE if staged else dil
        slope = slopes_ref[g * A_HEADS + h] * (float(dil) * LOG2E)

        col_minus_row = (lax.broadcasted_iota(jnp.int32, (qb, slab), 1)
                         - lax.broadcasted_iota(jnp.int32, (qb, slab), 0))
        for cls, off in enumerate((0, -A_RADIUS, qb - slab)):
            dist = jnp.abs(col_minus_row + off)
            bias_sc[cls, 0:qb, 0:slab] = jnp.where(dist <= A_RADIUS, -slope * dist.astype(F32), NEG_INF)

        def q_rows(q0, r, dil=dil, qb=qb, q_ref=q_ref):
            if dil == 1:
                return q_ref[pl.ds(pl.multiple_of(q0, 64), qb), :]
            return q_ref[pl.ds(q0 * dil + r, qb, stride=dil), :].astype(BF16)

        def kv_rows(ref, ks, r, dil=dil, slab=slab, staged=staged, kv_stride=kv_stride):
            if dil == 1:
                return ref[pl.ds(pl.multiple_of(ks, 64), slab), :]
            if staged:
                base = (r % A_STAGE) * (seq // A_STAGE) + r // A_STAGE
            else:
                base = r
            return ref[pl.ds(base + ks * kv_stride, slab, stride=kv_stride), :].astype(BF16)

        def body(it, carry, g=g, dil=dil, n=n, qb=qb, slab=slab, nqb=nqb, q_rows=q_rows, kv_rows=kv_rows,
                 k_ref=k_ref, v_ref=v_ref):
            blks = [it * A_UNROLL + u for u in range(A_UNROLL)]
            res = [blk // nqb for blk in blks]
            q0s = [(blk % nqb) * qb for blk in blks]
            kss = [jnp.clip(q0 - A_RADIUS, 0, n - slab) for q0 in q0s]
            scores = [lax.dot_general(q_rows(q0, r), kv_rows(k_ref, ks, r),
                                      (((1,), (1,)), ((), ())), preferred_element_type=F32)
                      for q0, ks, r in zip(q0s, kss, res)]
            probs, stats = [], []
            for s, q0 in zip(scores, q0s):
                cls = jnp.where(q0 == 0, 0, jnp.where(q0 + qb == n, 2, 1))
                s = s * scale + bias_sc[cls, 0:qb, 0:slab]
                m = jnp.max(s, axis=-1, keepdims=True)
                p = jnp.exp2(s - m)
                l = jnp.sum(p, axis=-1, keepdims=True)
                probs.append(p.astype(BF16))
                stats.append((m, l))
            outs = [jnp.dot(p, kv_rows(v_ref, ks, r), preferred_element_type=F32)
                    for p, ks, r in zip(probs, kss, res)]
            for o, (m, l), q0, r in zip(outs, stats, q0s, res):
                o = o * (1.0 / l)
                lse = jnp.broadcast_to(m + jnp.log(l) * LOG2E, (qb, LANES))
                if dil != 1:
                    rows = pl.ds(q0 * dil + r, qb, stride=dil)
                    o_sc[g - 1, rows, :] = o
                    lse_sc[g - 1, rows, :] = lse
                else:
                    rows = pl.ds(pl.multiple_of(q0, 64), qb)
                    l1, l2 = lse_sc[0, rows, :], lse_sc[1, rows, :]
                    mx = jnp.maximum(jnp.maximum(lse, l1), l2)
                    e0, e1, e2 = jnp.exp2(lse - mx), jnp.exp2(l1 - mx), jnp.exp2(l2 - mx)
                    y = (e0 * o + e1 * o_sc[0, rows, :] + e2 * o_sc[1, rows, :]) * (1.0 / (e0 + e1 + e2))
                    o_ref[0, rows, :] = (y * _silu(z_ref[0, rows, :].astype(F32))).astype(o_ref.dtype)
            return carry

        lax.fori_loop(0, (dil * nqb) // A_UNROLL, body, 0)


def attn_a(proj, proj_dil, slopes):
    b, l, _ = proj.shape
    ng = len(A_GROUPS)
    assert ng == 3 and (l // A_QB) % A_UNROLL == 0

    def spec(blk0):
        return pl.BlockSpec((1, l, HEAD_DIM), lambda i, h, blk0=blk0: (i, 0, blk0 + h))

    in_specs = [pl.BlockSpec(memory_space=pltpu.SMEM)]
    in_specs += [spec(s * A_HEADS) for s in range(3)]
    in_specs += [spec(c * A_HEADS) for c in range(3 * (ng - 1))]
    in_specs += [spec(PM_AZ // HEAD_DIM)]
    return pl.pallas_call(
        functools.partial(_attn_a_kernel, seq=l),
        grid=(b, A_HEADS),
        in_specs=in_specs,
        out_specs=pl.BlockSpec((1, l, HEAD_DIM), lambda i, h: (i, 0, h)),
        out_shape=jax.ShapeDtypeStruct((b, l, BRANCH_W), BF16),
        scratch_shapes=[pltpu.VMEM((3, A_QB, A_QB + 2 * A_RADIUS), F32),
                        pltpu.VMEM((ng - 1, l, LANES), F32), pltpu.VMEM((ng - 1, l, LANES), F32),
                        pltpu.VMEM((l, LANES), F32), pltpu.VMEM((l, LANES), F32)],
        compiler_params=_cparams(("parallel", "parallel")),
        name="attn_dilated",
    )(slopes, proj, proj, proj, *([proj_dil] * (3 * (ng - 1))), proj)


def _na_bias_tables(rpb, rows):
    kr = min(NA_KR, rows)
    qc = np.arange(GRID_W)
    cs = np.clip(qc - NA_KC // 2, 0, GRID_W - NA_KC)
    col_ok = (qc[None, :] >= cs[:, None]) & (qc[None, :] < cs[:, None] + NA_KC)
    dc = np.clip(qc[None, :] - qc[:, None] + NA_KC - 1, 0, 2 * NA_KC - 2)
    dc_onehot = (dc[..., None] == np.arange(2 * NA_KC - 1)).astype(np.float32)
    tabs = []
    for a in (0, 1, rows // NA_QROWS - 1):
        ks = int(np.clip(NA_QROWS * a - NA_QROWS, 0, rows - NA_KROWS))
        rq = NA_QROWS * a + np.arange(NA_QROWS)
        rs = np.clip(rq - kr // 2, 0, rows - kr)
        rk = ks + np.arange(NA_KROWS)
        row_ok = (rk[None, :] >= rs[:, None]) & (rk[None, :] < rs[:, None] + kr)
        dr = np.clip(rk[None, :] - rq[:, None] + NA_KR - 1, 0, 2 * NA_KR - 2)
        dr_onehot = (dr[..., None] == np.arange(2 * NA_KR - 1)).astype(np.float32)
        t = jnp.einsum("qjr,ckd,hrd->hqcjk", dr_onehot, dc_onehot, rpb.astype(F32),
                       precision=lax.Precision.HIGHEST)
        ok = row_ok[:, None, :, None] & col_ok[None, :, None, :]
        tabs.append(jnp.where(ok[None], t * LOG2E, NEG_INF).reshape(-1, NA_QT, NA_KT))
    return jnp.stack(tabs, axis=1)


NA_UNROLL = 4


def _attn_b_kernel(q_ref, k_ref, v_ref, z_ref, bias_ref, o_ref, *, seq):
    rows = seq // GRID_W
    nblk = rows // NA_QROWS
    scale = HEAD_DIM ** -0.5 * LOG2E

    def body(it, carry):
        blks = [it * NA_UNROLL + u for u in range(NA_UNROLL)]
        ks_rows = [jnp.clip(NA_QROWS * a - NA_QROWS, 0, rows - NA_KROWS) for a in blks]
        qrows = [pl.ds(pl.multiple_of(a * NA_QT, NA_QT), NA_QT) for a in blks]
        krows = [pl.ds(pl.multiple_of(ks * GRID_W, GRID_W), NA_KT) for ks in ks_rows]
        scores = [lax.dot_general(q_ref[0, qr, :], k_ref[0, kr, :], (((1,), (1,)), ((), ())),
                                  preferred_element_type=F32) for qr, kr in zip(qrows, krows)]
        probs, sums = [], []
        for s, a, ks in zip(scores, blks, ks_rows):
            s = s * scale + bias_ref[0, (NA_QROWS * a - ks) // NA_QROWS]
            p = jnp.exp2(s - jnp.max(s, axis=-1, keepdims=True))
            sums.append(jnp.sum(p, axis=-1, keepdims=True))
            probs.append(p.astype(BF16))
        outs = [jnp.dot(p, v_ref[0, kr, :], preferred_element_type=F32) for p, kr in zip(probs, krows)]
        for o, l, qr in zip(outs, sums, qrows):
            o_ref[0, qr, :] = (o * (1.0 / l) * _silu(z_ref[0, qr, :].astype(F32))).astype(o_ref.dtype)
        return carry

    lax.fori_loop(0, nblk // NA_UNROLL, body, 0)


def attn_b(proj, bias):
    b, l, _ = proj.shape
    assert l % (NA_QT * NA_UNROLL) == 0 and l // GRID_W >= NA_KROWS

    def col_spec(off):
        return pl.BlockSpec((1, l, HEAD_DIM), lambda h, i, off=off: (i, 0, off // HEAD_DIM + h))

    return pl.pallas_call(
        functools.partial(_attn_b_kernel, seq=l),
        grid=(B_HEADS, b),
        in_specs=[col_spec(PM_B), col_spec(PM_B + B_HEADS * HEAD_DIM),
                  col_spec(PM_B + 2 * B_HEADS * HEAD_DIM), col_spec(PM_BZ),
                  pl.BlockSpec((1, 3, NA_QT, NA_KT), lambda h, i: (h, 0, 0, 0))],
        out_specs=pl.BlockSpec((1, l, HEAD_DIM), lambda h, i: (i, 0, h)),
        out_shape=jax.ShapeDtypeStruct((b, l, BRANCH_W), BF16),
        compiler_params=_cparams(("parallel", "parallel")),
        name="attn_neighbourhood",
    )(proj, proj, proj, proj, bias)


HY_R = 4
HY_ND = 2 * HY_R - 1
HY_ROW0 = 16
HY_RC = 16


def _shift_down(x):
    t = lax.broadcasted_iota(jnp.int32, x.shape, 0)
    return jnp.where(t == 0, 0.0, pltpu.roll(x, 1, axis=0))


def _shift_up(x):
    n = x.shape[0]
    t = lax.broadcasted_iota(jnp.int32, x.shape, 0)
    return jnp.where(t == n - 1, 0.0, pltpu.roll(x, n - 1, axis=0))


def _conv3_kernel(u_ref, w_ref, b_ref, o_ref, *, sub):
    xs = [u_ref[0, pl.ds(p, sub, stride=HY_R), :] for p in range(HY_R)]
    w0, w1, w2, bias = w_ref[0:1, :], w_ref[1:2, :], w_ref[2:3, :], b_ref[...]
    for p in range(HY_R):
        prev = xs[p - 1] if p > 0 else _shift_down(xs[HY_R - 1])
        nxt = xs[p + 1] if p < HY_R - 1 else _shift_up(xs[0])
        o_ref[0, p] = (w0 * prev + w1 * xs[p] + w2 * nxt + bias).astype(o_ref.dtype)


def conv3(side, conv_w, conv_b):
    b, l, _ = side.shape
    sub = l // HY_R
    return pl.pallas_call(
        functools.partial(_conv3_kernel, sub=sub),
        grid=(b, C_PROJ // LANES),
        in_specs=[pl.BlockSpec((1, l, LANES), lambda i, j: (i, 0, SIDE_C // LANES + j)),
                  pl.BlockSpec((3, LANES), lambda i, j: (0, j)),
                  pl.BlockSpec((1, LANES), lambda i, j: (0, j))],
        out_specs=pl.BlockSpec((1, HY_R, sub, LANES), lambda i, j: (i, 0, 0, j)),
        out_shape=jax.ShapeDtypeStruct((b, HY_R, sub, C_PROJ), BF16),
        compiler_params=_cparams(("parallel", "parallel")),
        name="hyena_conv3",
    )(side, conv_w, conv_b.reshape(1, C_PROJ))


def _filter_kernel(feats_ref, w1_ref, b1_ref, w2_ref, b2_ref, freq_ref, w3_ref, dec_ref, o_ref, hid_ref,
                   *, tc):
    j = pl.program_id(0)
    hp = lax.Precision.HIGHEST

    @pl.when(j == 0)
    def _():
        h1 = jnp.sin(freq_ref[0:1, :] * (jnp.dot(feats_ref[...], w1_ref[...], precision=hp,
                                                 preferred_element_type=F32) + b1_ref[...]))
        hid_ref[...] = jnp.sin(freq_ref[1:2, :] * (jnp.dot(h1, w2_ref[...], precision=hp,
                                                           preferred_element_type=F32) + b2_ref[...]))

    raw = jnp.dot(hid_ref[...].astype(BF16), w3_ref[...].astype(BF16), preferred_element_type=F32)
    t_norm = feats_ref[:, 0:1]
    hfil = raw * jnp.exp(-t_norm * jnp.exp(dec_ref[...]))
    hfil = hfil * (1.0 / (jnp.sum(jnp.abs(hfil), axis=0, keepdims=True) + 1e-6))
    dropped_row = ((j * tc) // HY_CH) % 2 - 1
    t = lax.broadcasted_iota(jnp.int32, hfil.shape, 0)
    o_ref[...] = jnp.where(t == dropped_row, 0.0, hfil).astype(o_ref.dtype)


def hyena_filter_taps(seq, w1, b1, w2, b2, w3, freq, log_decay):
    t = jnp.concatenate([jnp.arange(p, seq, HY_R) for p in range(HY_R)]).astype(F32)
    t_norm = t / (seq - 1)
    ang = (2.0 * math.pi / seq) * t
    bands = jnp.linspace(1e-4, HY_BANDS - 1, HY_BANDS, dtype=F32)
    feats = jnp.concatenate([t_norm[:, None], jnp.cos(ang[:, None] * bands), -jnp.sin(ang[:, None] * bands)], axis=-1)
    emb = 64
    feats = jnp.pad(feats, ((0, 0), (0, emb - HY_EMB)))
    w1p = jnp.pad(w1, ((0, emb - HY_EMB), (0, 0)))
    ncol = HY_ORDER * 2 * HY_CH
    tc = 256
    full = lambda shape: pl.BlockSpec(shape, lambda j: (0, 0))
    return pl.pallas_call(
        functools.partial(_filter_kernel, tc=tc),
        grid=(ncol // tc,),
        in_specs=[full((seq, emb)), full((emb, HY_FFN)), full((1, HY_FFN)), full((HY_FFN, HY_FFN)),
                  full((1, HY_FFN)), full((2, HY_FFN)),
                  pl.BlockSpec((HY_FFN, tc), lambda j: (0, j)),
                  pl.BlockSpec((1, tc), lambda j: (0, j))],
        out_specs=pl.BlockSpec((seq, tc), lambda j: (0, j)),
        out_shape=jax.ShapeDtypeStruct((seq, ncol), BF16),
        scratch_shapes=[pltpu.VMEM((seq, HY_FFN), F32)],
        compiler_params=_cparams(("arbitrary",)),
        name="hyena_filter_taps",
    )(feats, w1p, b1.reshape(1, HY_FFN), w2, b2.reshape(1, HY_FFN), freq, w3, log_decay.reshape(1, ncol))


def _dft_consts(seq):
    sub = seq // HY_R
    k = jnp.arange(sub, dtype=jnp.int32)
    ang = ((k[:, None] * k[None, :]) % (2 * sub)).astype(F32) * (math.pi / sub)
    cmat = jnp.cos(ang)
    smat = jnp.sin(ang)
    sign = (1 - 2 * (k % 2)).astype(F32)
    smat = jnp.where((k == 0)[:, None], sign[None, :], smat)
    tw = k.astype(F32) * (math.pi / sub)
    cw = jnp.broadcast_to(jnp.cos(tw)[:, None], (sub, LANES))
    sw = jnp.broadcast_to(jnp.sin(tw)[:, None], (sub, LANES))
    return dict(cmat=cmat.astype(BF16), smat=smat.astype(BF16), smat_t=smat.T.astype(BF16), cw=cw, sw=sw)


def _residue_transforms(c_ref, s_ref, x_ref):
    cm, sm = c_ref[...], s_ref[...]
    dot = lambda a, b: jnp.dot(a, b, preferred_element_type=F32)
    return [(dot(cm, x_ref[p]), dot(sm, x_ref[p])) for p in range(HY_R)]


def _filter_spec_kernel(c_ref, s_ref, cw_ref, sw_ref, xf_ref, xb_ref, tab_ref, row0_ref, *, seq, tc):
    i = pl.program_id(2)
    fwd = _residue_transforms(c_ref, s_ref, xf_ref)
    bwd = _residue_transforms(c_ref, s_ref, xb_ref)
    c = jnp.tile(cw_ref[...], (1, tc // LANES))
    s = jnp.tile(sw_ref[...], (1, tc // LANES))
    inv_n = 1.0 / (2 * seq)
    sc = HY_R * 2.0 * inv_n
    mid = HY_R - 1

    def put(d, tr, ti):
        tab_ref[0, 2 * (d + mid)] = sc * tr
        tab_ref[0, 2 * (d + mid) + 1] = sc * ti

    put(0, fwd[0][0] + bwd[0][0], bwd[0][1] - fwd[0][1])
    for d in range(1, HY_R):
        (pf, qf), (pb, qb) = fwd[HY_R - d], bwd[d]
        put(d, c * pf - s * qf + pb, qb - c * qf - s * pf)
        (pf, qf), (pb, qb) = fwd[d], bwd[HY_R - d]
        put(-d, pf + c * pb - s * qb, c * qb + s * pb - qf)

    @pl.when(i == 0)
    def _():
        row0_ref[0] = jnp.zeros(row0_ref.shape[1:], F32)
        rn = HY_R * inv_n

        def put0(d, bin0, bin_m):
            row0_ref[0, d + mid:d + mid + 1, :] = rn * bin0
            row0_ref[0, 8 + d + mid:8 + d + mid + 1, :] = rn * bin_m

        put0(0, fwd[0][0][0:1] + bwd[0][0][0:1], fwd[0][1][0:1] + bwd[0][1][0:1])
        for d in range(1, HY_R):
            put0(d, fwd[HY_R - d][0][0:1] + bwd[d][0][0:1], bwd[d][1][0:1] - fwd[HY_R - d][1][0:1])
            put0(-d, fwd[d][0][0:1] + bwd[HY_R - d][0][0:1], fwd[d][1][0:1] - bwd[HY_R - d][1][0:1])


def hyena_filter_spectrum(taps, dc):
    seq = taps.shape[0]
    sub = seq // HY_R
    taps = taps.reshape(HY_R, sub, taps.shape[1])
    tf, tc = _tile(sub, 256), 256
    ncb = HY_CH // tc
    mspec = pl.BlockSpec((tf, sub), lambda o, j, i: (i, 0))
    tspec = pl.BlockSpec((tf, LANES), lambda o, j, i: (i, 0))
    return pl.pallas_call(
        functools.partial(_filter_spec_kernel, seq=seq, tc=tc),
        grid=(HY_ORDER, ncb, sub // tf),
        in_specs=[mspec, mspec, tspec, tspec,
                  pl.BlockSpec((HY_R, sub, tc), lambda o, j, i: (0, 0, (o * 2) * ncb + j)),
                  pl.BlockSpec((HY_R, sub, tc), lambda o, j, i: (0, 0, (o * 2 + 1) * ncb + j))],
        out_specs=[pl.BlockSpec((1, 2 * HY_ND, tf, tc), lambda o, j, i: (o, 0, i, j)),
                   pl.BlockSpec((1, HY_ROW0, tc), lambda o, j, i: (o, 0, j))],
        out_shape=[jax.ShapeDtypeStruct((HY_ORDER, 2 * HY_ND, sub, HY_CH), F32),
                   jax.ShapeDtypeStruct((HY_ORDER, HY_ROW0, HY_CH), F32)],
        compiler_params=_cparams(("parallel", "parallel", "arbitrary")),
        name="hyena_filter_spectrum",
    )(dc["cmat"], dc["smat"], dc["cw"], dc["sw"], taps, taps)


def _conv_fwd_kernel(c_ref, s_ref, z_ref, tab_ref, row0_ref, uv_ref, pq_ref, *, tf, tc):
    i = pl.program_id(1)
    cm, sm = c_ref[...], s_ref[...]
    dot = lambda a, b: jnp.dot(a, b, preferred_element_type=F32)
    mid = HY_R - 1
    halves = [slice(0, tc // 2), slice(tc // 2, tc)]

    def transforms(lanes):
        for p in range(HY_R):
            x = z_ref[0, p, :, lanes]
            pq_ref[2 * p, :, lanes] = dot(cm, x)
            pq_ref[2 * p + 1, :, lanes] = dot(sm, x)

    def products(lanes):
        for r in range(tf // HY_RC):
            rows = slice(r * HY_RC, (r + 1) * HY_RC)
            for q in range(HY_R):
                u = v = None
                for p in range(HY_R):
                    pp, qp = pq_ref[2 * p, rows, lanes], pq_ref[2 * p + 1, rows, lanes]
                    tr = tab_ref[0, 2 * (p - q + mid), rows, lanes]
                    ti = tab_ref[0, 2 * (p - q + mid) + 1, rows, lanes]
                    du, dv = pp * tr + qp * ti, qp * tr - pp * ti
                    u, v = (du, dv) if u is None else (u + du, v + dv)
                uv_ref[0, 2 * q, rows, lanes] = u.astype(uv_ref.dtype)
                uv_ref[0, 2 * q + 1, rows, lanes] = v.astype(uv_ref.dtype)

    transforms(halves[0])
    transforms(halves[1])
    products(halves[0])
    products(halves[1])

    @pl.when(i == 0)
    def _():
        head = slice(0, HY_RC)
        first = lax.broadcasted_iota(jnp.int32, (HY_RC, 1), 0) == 0
        for q in range(HY_R):
            u0 = v0 = None
            for p in range(HY_R):
                du = pq_ref[2 * p, 0:1, :] * row0_ref[0, p - q + mid:p - q + mid + 1, :]
                dv = pq_ref[2 * p + 1, 0:1, :] * row0_ref[0, 8 + p - q + mid:8 + p - q + mid + 1, :]
                u0, v0 = (du, dv) if u0 is None else (u0 + du, v0 + dv)
            uv_ref[0, 2 * q, head, :] = jnp.where(first, u0, uv_ref[0, 2 * q, head, :].astype(F32)).astype(uv_ref.dtype)
            uv_ref[0, 2 * q + 1, head, :] = jnp.where(first, v0, uv_ref[0, 2 * q + 1, head, :].astype(F32)).astype(uv_ref.dtype)


def conv_fwd(z, zcol, dc, tabs, row0, order):
    b, _, sub, _ = z.shape
    tf, tc = _tile(sub, 256), 512
    ncb = HY_CH // tc
    mspec = pl.BlockSpec((tf, sub), lambda j, i, n: (i, 0))
    return pl.pallas_call(
        functools.partial(_conv_fwd_kernel, tf=tf, tc=tc),
        grid=(ncb, sub // tf, b),
        in_specs=[mspec, mspec,
                  pl.BlockSpec((1, HY_R, sub, tc), lambda j, i, n: (n, 0, 0, zcol // tc + j)),
                  pl.BlockSpec((1, 2 * HY_ND, tf, tc), lambda j, i, n: (order, 0, i, j)),
                  pl.BlockSpec((1, HY_ROW0, tc), lambda j, i, n: (order, 0, j))],
        out_specs=pl.BlockSpec((1, 2 * HY_R, tf, tc), lambda j, i, n: (n, 0, i, j)),
        out_shape=jax.ShapeDtypeStruct((b, 2 * HY_R, sub, HY_CH), BF16),
        scratch_shapes=[pltpu.VMEM((2 * HY_R, tf, tc), F32)],
        compiler_params=_cparams(("parallel", "parallel", "arbitrary")),
        name="hyena_conv_fwd",
    )(dc["cmat"], dc["smat"], z, tabs, row0)


def _conv_inv_kernel(c_ref, st_ref, uv_ref, x_ref, zp_ref, skip_ref, *rest, gated, tt, tc):
    cm, sm = c_ref[...], st_ref[...]
    dot = lambda a, b: jnp.dot(a, b, preferred_element_type=F32)
    skip = skip_ref[0]
    parts = []
    for q in range(HY_R):
        y = dot(cm, uv_ref[0, 2 * q]) + dot(sm, uv_ref[0, 2 * q + 1])
        parts.append(x_ref[0, q].astype(F32) * (y + zp_ref[0, q].astype(F32) * skip))
    if not gated:
        o_ref = rest[0]
        for q in range(HY_R):
            o_ref[0, q] = parts[q].astype(o_ref.dtype)
        return
    gate_ref, o_ref, il_ref = rest
    for s in range(tc // LANES):
        lanes = slice(s * LANES, (s + 1) * LANES)
        for q in range(HY_R):
            il_ref[s, pl.ds(q, tt, stride=HY_R), :] = parts[q][:, lanes]
    for s in range(tc // LANES):
        lanes = slice(s * LANES, (s + 1) * LANES)
        o_ref[0, :, lanes] = (il_ref[s] * _silu(gate_ref[0, :, lanes].astype(F32))).astype(o_ref.dtype)


def conv_inv(uv, dc, xmul, xcol, zprev, zcol, skip, gate=None, gcol=0):
    b, _, sub, _ = uv.shape
    tt, tc = _tile(sub, 256), 512
    ncb = HY_CH // tc
    mspec = pl.BlockSpec((tt, sub), lambda n, j, i: (i, 0))

    def tile_spec(col):
        return pl.BlockSpec((1, HY_R, tt, tc), lambda n, j, i, col=col: (n, 0, i, col // tc + j))

    in_specs = [mspec, mspec, pl.BlockSpec((1, 2 * HY_R, sub, tc), lambda n, j, i: (n, 0, 0, j)),
                tile_spec(xcol), tile_spec(zcol), pl.BlockSpec((1, 1, tc), lambda n, j, i: (0, 0, j))]
    args = [dc["cmat"], dc["smat_t"], uv, xmul, zprev, skip.reshape(1, 1, HY_CH)]
    scratch = []
    if gate is None:
        out_spec = pl.BlockSpec((1, HY_R, tt, tc), lambda n, j, i: (n, 0, i, j))
        out_shape = jax.ShapeDtypeStruct((b, HY_R, sub, HY_CH), BF16)
    else:
        in_specs.append(pl.BlockSpec((1, HY_R * tt, tc), lambda n, j, i: (n, i, gcol // tc + j)))
        args.append(gate)
        out_spec = pl.BlockSpec((1, HY_R * tt, tc), lambda n, j, i: (n, i, j))
        out_shape = jax.ShapeDtypeStruct((b, HY_R * sub, HY_CH), BF16)
        scratch = [pltpu.VMEM((tc // LANES, HY_R * tt, LANES), F32)]
    return pl.pallas_call(
        functools.partial(_conv_inv_kernel, gated=gate is not None, tt=tt, tc=tc),
        grid=(b, ncb, sub // tt),
        in_specs=in_specs,
        out_specs=out_spec,
        out_shape=out_shape,
        scratch_shapes=scratch,
        compiler_params=_cparams(("parallel", "parallel", "arbitrary")),
        name="hyena_conv_inv",
    )(*args)


def mixer_hyena(proj, side, dc, tabs, row0, conv_w, conv_b, skip):
    u = conv3(side, conv_w, conv_b)
    uv = conv_fwd(u, 0, dc, tabs, row0, 0)
    z1 = conv_inv(uv, dc, u, HY_CH, u, 0, skip[0])
    uv = conv_fwd(z1, 0, dc, tabs, row0, 1)
    return conv_inv(uv, dc, u, 2 * HY_CH, z1, 0, skip[1], gate=proj, gcol=PM_CZ)


def _merge_kernel(ya_ref, yb_ref, yc_ref, w_ref, g0_ref, g1_ref, g2_ref, o_ref):
    acc = None
    for y_ref, g_ref, n in ((ya_ref, g0_ref, 0), (yb_ref, g1_ref, 1), (yc_ref, g2_ref, 2)):
        t = _sigmoid(g_ref[...].astype(F32)) * jnp.dot(y_ref[...], w_ref[0, n], preferred_element_type=F32)
        acc = t if acc is None else acc + t
    o_ref[...] = acc.astype(o_ref.dtype)


def merge_branches(ya, yb, yc, w_branch, layer, proj2d, d):
    m = ya.shape[0]
    tm, tn = _tile(m, 1024), _tile(d, 1024)
    yspec = pl.BlockSpec((tm, BRANCH_W), lambda i, j: (i, 0))

    def gspec(n):
        return pl.BlockSpec((tm, tn), lambda i, j, n=n: (i, (PM_G + n * d) // tn + j))

    return pl.pallas_call(
        _merge_kernel,
        grid=(m // tm, d // tn),
        in_specs=[yspec, yspec, yspec,
                  pl.BlockSpec((1, N_BRANCH, BRANCH_W, tn), lambda i, j: (layer, 0, 0, j)),
                  gspec(0), gspec(1), gspec(2)],
        out_specs=pl.BlockSpec((tm, tn), lambda i, j: (i, j)),
        out_shape=jax.ShapeDtypeStruct((m, d), BF16),
        compiler_params=_cparams(("parallel", "arbitrary")),
        name="branch_merge",
    )(ya, yb, yc, w_branch, proj2d, proj2d, proj2d)


def _out_kernel(a_ref, w_ref, x_ref, mod_ref, o_ref):
    acc = jnp.dot(a_ref[0], w_ref[0], preferred_element_type=F32)
    o_ref[0] = x_ref[0] + mod_ref[0, 2:3, :] * acc


def out_proj(mixed, w_out, layer, x, mod3):
    b, l, d = x.shape
    tm, tn = _tile(l, 1024), _tile(d, 1024)
    return pl.pallas_call(
        _out_kernel,
        grid=(b, l // tm, d // tn),
        in_specs=[pl.BlockSpec((1, tm, d), lambda n, i, j: (n, i, 0)),
                  pl.BlockSpec((1, d, tn), lambda n, i, j: (layer, 0, j)),
                  pl.BlockSpec((1, tm, tn), lambda n, i, j: (n, i, j)),
                  pl.BlockSpec((1, 3, tn), lambda n, i, j: (n, 0, j))],
        out_specs=pl.BlockSpec((1, tm, tn), lambda n, i, j: (n, i, j)),
        out_shape=jax.ShapeDtypeStruct((b, l, d), F32),
        compiler_params=_cparams(("parallel", "parallel", "arbitrary")),
        name="out_proj",
    )(mixed, w_out, x, mod3)


def _trunk(x, mods, layers, w_in, w_branch, w_out, dc, slopes, final_g):
    b, l, d = x.shape
    for layer, (lw, mod3) in enumerate(zip(layers, mods)):
        h = norm_mod(x, lw["norm_g"], mod3)
        proj2d, side2d = in_proj(h.reshape(b * l, d), w_in, layer, d)
        proj = proj2d.reshape(b, l, -1)
        side = side2d.reshape(b, l, -1)
        ya = attn_a(proj, side, slopes)
        yb = attn_b(proj, _na_bias_tables(lw["na_rpb"], l // GRID_W))
        taps = hyena_filter_taps(l, lw["hy_w1"], lw["hy_b1"], lw["hy_w2"], lw["hy_b2"], lw["hy_w3"],
                                 lw["hy_freq"], lw["hy_log_decay"])
        tabs, row0 = hyena_filter_spectrum(taps, dc)
        yc = mixer_hyena(proj, side, dc, tabs, row0, lw["conv_w"], lw["conv_b"], lw["hy_skip"])
        mixed = merge_branches(ya.reshape(b * l, -1), yb.reshape(b * l, -1), yc.reshape(b * l, -1),
                               w_branch, layer, proj2d, d)
        x = out_proj(mixed.reshape(b, l, d), w_out, layer, x, mod3)
    return final_norm(x, final_g)


def kernel(x_prompt, x_sample, c_prompt, c_sample, norm_g, w_ada, b_ada, w_in, w_branch, w_out, na_rpb, conv_w, conv_b, hy_w1, hy_b1, hy_w2, hy_b2, hy_w3, hy_freq, hy_log_decay, hy_skip, final_g):
    depth, d = norm_g.shape
    nbp = c_prompt.shape[0]
    c_all = jnp.concatenate([c_prompt, c_sample], axis=0)
    nslope = len(A_GROUPS) * A_HEADS
    slopes = jnp.asarray(2.0 ** (-8.0 * (np.arange(nslope) + 1) / nslope), F32)
    layers, mods_p, mods_s = [], [], []
    for i in range(depth):
        mod3 = ada_mod(c_all, w_ada, b_ada, i).reshape(c_all.shape[0], 3, d)
        mods_p.append(mod3[:nbp])
        mods_s.append(mod3[nbp:])
        layers.append(dict(
            norm_g=norm_g[i], na_rpb=na_rpb[i], conv_w=conv_w[i], conv_b=conv_b[i],
            hy_w1=hy_w1[i], hy_b1=hy_b1[i], hy_w2=hy_w2[i], hy_b2=hy_b2[i], hy_w3=hy_w3[i],
            hy_freq=hy_freq[i], hy_log_decay=hy_log_decay[i], hy_skip=hy_skip[i]))
    w_branch_b, w_out_b = w_branch.astype(BF16), w_out.astype(BF16)
    outs = []
    for x, mods in ((x_prompt, mods_p), (x_sample, mods_s)):
        outs.append(_trunk(x, mods, layers, w_in, w_branch_b, w_out_b, _dft_consts(x.shape[1]), slopes, final_g))
    return tuple(outs)
```
